```python
import jax, jax.numpy as jnp
from jax import lax
import numpy as np

D_MODEL = 1024
BATCH = 8
SEQ = 8192
DEPTH = 2
DEC_BATCH = 4
DEC_SEQ = 8192
PAST_LEN = 128

D_MIX = D_MODEL
D_A = D_MIX // 2
D_B = D_MIX - D_A
HEAD_DIM = 64
H_A = D_A // HEAD_DIM
H_B = D_B // HEAD_DIM
CHUNK = 128
GRID_W = 64
WIN_H = 8
WIN_W = 16
N_EXPERTS = 16
N_GROUPS = 4
E_PER_GROUP = N_EXPERTS // N_GROUPS
TOP_K = 2
D_EXPERT = D_MODEL // 2
D_IN = 2 * D_A + 3 * D_B
EPS = 1e-6

kernel_name = "hymba_gmlp_natten_grouped_moe_encoder"


def rmsnorm(x, g):
    xf = x.astype(jnp.float32)
    y = xf * lax.rsqrt(jnp.mean(xf * xf, axis=-1, keepdims=True) + EPS)
    return (y * g.astype(jnp.float32)).astype(x.dtype)


def gmlp_mixer(z_a, g_v, ws, bs):
    B, T, _ = z_a.shape
    z = jax.nn.gelu(z_a)
    u, v = z[..., :D_A], z[..., D_A:]
    v = rmsnorm(v, g_v).reshape(B, T // CHUNK, CHUNK, H_A, HEAD_DIM)
    sv = jnp.einsum('hpq,bcqhd->bcphd', ws, v) + bs.T[None, None, :, :, None]
    return u * sv.reshape(B, T, D_A)


def neighbourhood_attention(qkv, rpb):
    B, T, _ = qkv.shape
    rows = T // GRID_W
    kh = min(WIN_H, rows)
    q, k, v = jnp.split(qkv, 3, axis=-1)
    shp = (B, rows, GRID_W, H_B, HEAD_DIM)
    q, k, v = q.reshape(shp), k.reshape(shp), v.reshape(shp)

    r = jnp.arange(rows)
    rs = jnp.clip(r - kh // 2, 0, rows - kh)
    row_abs = rs[:, None] + jnp.arange(kh)[None, :]
    row_rel = row_abs - r[:, None] + (WIN_H - 1)
    c = jnp.arange(GRID_W)
    cs = jnp.clip(c - WIN_W // 2, 0, GRID_W - WIN_W)
    col_mask = (c[None, :] >= cs[:, None]) & (c[None, :] < cs[:, None] + WIN_W)
    col_rel = jnp.clip(c[None, :] - c[:, None], -(WIN_W - 1), WIN_W - 1) + (WIN_W - 1)
    bias = rpb[:, row_rel[:, None, :, None], col_rel[None, :, None, :]].astype(jnp.float32)
    mask = col_mask[None, None, :, None, :]
    scale = HEAD_DIM ** -0.5

    def one_seq(args):
        qs, ks, vs = args
        k_rows = ks[row_abs]
        v_rows = vs[row_abs]
        s = jnp.einsum('rchd,rikhd->hrcik', qs, k_rows).astype(jnp.float32) * scale + bias
        s = jnp.where(mask, s, -jnp.inf)
        p = jax.nn.softmax(s.reshape(H_B, rows, GRID_W, kh * GRID_W), axis=-1)
        p = p.reshape(s.shape).astype(vs.dtype)
        return jnp.einsum('hrcik,rikhd->rchd', p, v_rows)

    out = lax.map(one_seq, (q, k, v))
    return out.reshape(B, T, D_B)


def grouped_moe(x_n, router_w, router_bias, w_gate, w_up, w_down):
    B, T, D = x_n.shape
    xt = x_n.reshape(B * T, D)
    logits = (xt @ router_w).astype(jnp.float32) + router_bias.astype(jnp.float32)
    probs = jax.nn.softmax(logits, axis=-1)
    grp = probs.reshape(-1, N_GROUPS, E_PER_GROUP)
    grp_score = lax.top_k(grp, TOP_K)[0].sum(-1)
    g_sel = jnp.argmax(grp_score, axis=-1)
    in_group = (jnp.arange(N_EXPERTS) // E_PER_GROUP)[None, :] == g_sel[:, None]
    vals, idx = lax.top_k(jnp.where(in_group, probs, -1.0), TOP_K)
    gates = vals / jnp.sum(vals, axis=-1, keepdims=True)
    combine = jnp.einsum('nk,nke->ne', gates, jax.nn.one_hot(idx, N_EXPERTS, dtype=jnp.float32))
    combine = combine.astype(xt.dtype)
    y = jnp.zeros_like(xt)
    for e in range(N_EXPERTS):
        h = jax.nn.silu(xt @ w_gate[e]) * (xt @ w_up[e])
        y = y + combine[:, e:e + 1] * (h @ w_down[e])
    return y.reshape(B, T, D)


def trunk(x, norm_mix, w_in, gmlp_v_norm, gmlp_ws, gmlp_bs, na_rpb, out_norm_a, out_norm_b,
          w_out, norm_ffn, router_w, router_bias, w_gate, w_up, w_down, norm_final):
    for l in range(DEPTH):
        xn = rmsnorm(x, norm_mix[l])
        z = xn @ w_in[l]
        ya = gmlp_mixer(z[..., :2 * D_A], gmlp_v_norm[l], gmlp_ws[l], gmlp_bs[l])
        yb = neighbourhood_attention(z[..., 2 * D_A:], na_rpb[l])
        y = jnp.concatenate([rmsnorm(ya, out_norm_a[l]), rmsnorm(yb, out_norm_b[l])], axis=-1)
        x = x + y @ w_out[l]
        x = x + grouped_moe(rmsnorm(x, norm_ffn[l]), router_w, router_bias,
                            w_gate[l], w_up[l], w_down[l])
    return rmsnorm(x, norm_final)


def setup_inputs(seed: int = 0) -> dict:
    key = jax.random.key(seed)
    ks = jax.random.split(key, 20)
    f32 = jnp.float32
    nrm = lambda k, shape, s: jax.random.normal(k, shape, f32) * s
    gain = lambda k, shape: 1.0 + 0.05 * jax.random.normal(k, shape, f32)
    return {
        "x_prompt": jax.random.normal(ks[0], (BATCH, SEQ, D_MODEL), f32),
        "x_sample": jax.random.normal(ks[1], (DEC_BATCH, DEC_SEQ, D_MODEL), f32),
        "norm_mix": gain(ks[2], (DEPTH, D_MODEL)),
        "w_in": nrm(ks[3], (DEPTH, D_MODEL, D_IN), D_MODEL ** -0.5),
        "gmlp_v_norm": gain(ks[4], (DEPTH, D_A)),
        "gmlp_ws": nrm(ks[5], (DEPTH, H_A, CHUNK, CHUNK), CHUNK ** -0.5),
        "gmlp_bs": 1.0 + 0.1 * jax.random.normal(ks[6], (DEPTH, H_A, CHUNK), f32),
        "na_rpb": nrm(ks[7], (DEPTH, H_B, 2 * WIN_H - 1, 2 * WIN_W - 1), 0.1),
        "out_norm_a": gain(ks[8], (DEPTH, D_A)),
        "out_norm_b": gain(ks[9], (DEPTH, D_B)),
        "w_out": nrm(ks[10], (DEPTH, D_MIX, D_MODEL), 0.5 * D_MIX ** -0.5),
        "norm_ffn": gain(ks[11], (DEPTH, D_MODEL)),
        "router_w": nrm(ks[12], (D_MODEL, N_EXPERTS), D_MODEL ** -0.5),
        "router_bias": nrm(ks[13], (N_EXPERTS,), 0.01),
        "w_gate": nrm(ks[14], (DEPTH, N_EXPERTS, D_MODEL, D_EXPERT), D_MODEL ** -0.5),
        "w_up": nrm(ks[15], (DEPTH, N_EXPERTS, D_MODEL, D_EXPERT), D_MODEL ** -0.5),
        "w_down": nrm(ks[16], (DEPTH, N_EXPERTS, D_EXPERT, D_MODEL), 0.5 * D_EXPERT ** -0.5),
        "norm_final": gain(ks[17], (D_MODEL,)),
    }


def reference(x_prompt, x_sample, norm_mix, w_in, gmlp_v_norm, gmlp_ws, gmlp_bs, na_rpb,
              out_norm_a, out_norm_b, w_out, norm_ffn, router_w, router_bias,
              w_gate, w_up, w_down, norm_final):
    y_prompt = trunk(x_prompt, norm_mix, w_in, gmlp_v_norm, gmlp_ws, gmlp_bs, na_rpb, out_norm_a,
                     out_norm_b, w_out, norm_ffn, router_w, router_bias, w_gate, w_up, w_down,
                     norm_final)
    y_sample = trunk(x_sample, norm_mix, w_in, gmlp_v_norm, gmlp_ws, gmlp_bs, na_rpb, out_norm_a,
                     out_norm_b, w_out, norm_ffn, router_w, router_bias, w_gate, w_up, w_down,
                     norm_final)
    return (y_prompt, y_sample)
```

```python
import functools

import jax
import jax.numpy as jnp
import numpy as np
from jax import lax
from jax.experimental import pallas as pl
from jax.experimental.pallas import tpu as pltpu

F32 = jnp.float32
BF16 = jnp.bfloat16

D_MODEL = 1024
DEPTH = 2
D_A = 512
D_B = 512
HEAD_DIM = 64
H_A = D_A // HEAD_DIM
H_B = D_B // HEAD_DIM
CHUNK = 128
GRID_W = 64
WIN_H = 8
WIN_W = 16
N_EXPERTS = 16
N_GROUPS = 4
E_PER_GROUP = 4
D_EXPERT = 512
D_IN = 2 * D_A + 3 * D_B
EPS = 1e-6
NEG = -1e30

PAIRS = ((0, 1), (0, 2), (0, 3), (1, 2), (1, 3), (2, 3))
N_PAIRS = len(PAIRS)
N_CLASSES = N_GROUPS * N_PAIRS

TM_TOK = 512
ROWS_BLK = 8
TQ = ROWS_BLK * GRID_W
TM_MOE = 256
VMEM_LIMIT = 56 * 1024 * 1024


def _rms(x, g):
    return x * lax.rsqrt(jnp.mean(x * x, axis=-1, keepdims=True) + EPS) * g


def _dot(a, b):
    return jnp.dot(a, b, preferred_element_type=F32)


def _dot_nt(a, b):
    return lax.dot_general(a, b, (((1,), (1,)), ((), ())), preferred_element_type=F32)


def _in_proj_kernel(x_ref, nm_ref, win_ref, gv_ref, wsp_ref, bsb_ref, ona_ref, ya_ref, qkv_ref, ybuf):
    x = x_ref[...]
    xn = _rms(x, nm_ref[...])
    z = _dot(xn.astype(BF16), win_ref[...])
    qkv_ref[:, 0:D_B] = (z[:, 2 * D_A:2 * D_A + D_B] * (HEAD_DIM ** -0.5)).astype(BF16)
    qkv_ref[:, D_B:3 * D_B] = z[:, 2 * D_A + D_B:].astype(BF16)
    za = jax.nn.gelu(z[:, :2 * D_A])
    u = za[:, :D_A]
    vn = _rms(za[:, D_A:], gv_ref[...]).astype(BF16)
    lane = lax.broadcasted_iota(jnp.int32, (CHUNK, 2 * HEAD_DIM), 1)
    is_lo = lane < HEAD_DIM
    zero = jnp.zeros((CHUNK, 2 * HEAD_DIM), BF16)
    tm = x.shape[0]
    for c2 in range(tm // (2 * CHUNK)):
        for j in range(H_A // 2):
            cols = []
            for cc in range(2):
                r0 = (2 * c2 + cc) * CHUNK
                vp = vn[r0:r0 + CHUNK, j * 128:(j + 1) * 128]
                cols.append(jnp.concatenate([jnp.where(is_lo, vp, zero), jnp.where(is_lo, zero, vp)], axis=0))
            rhs = jnp.concatenate(cols, axis=1)
            sv = _dot(wsp_ref[j], rhs)
            for cc in range(2):
                r0 = (2 * c2 + cc) * CHUNK
                ybuf[r0:r0 + CHUNK, j * 128:(j + 1) * 128] = u[r0:r0 + CHUNK, j * 128:(j + 1) * 128] * (
                    sv[:, cc * 128:(cc + 1) * 128] + bsb_ref[:, j * 128:(j + 1) * 128])
    ya_ref[...] = _rms(ybuf[...], ona_ref[...]).astype(BF16)


def _in_proj(x, n, nm, win, gv, wsp, bsb, ona):
    const = lambda i: (0, 0)
    return pl.pallas_call(
        _in_proj_kernel,
        grid=(n // TM_TOK,),
        in_specs=[
            pl.BlockSpec((TM_TOK, D_MODEL), lambda i: (i, 0)),
            pl.BlockSpec((1, D_MODEL), const),
            pl.BlockSpec((D_MODEL, D_IN), const),
            pl.BlockSpec((1, D_A), const),
            pl.BlockSpec((H_A // 2, CHUNK, 2 * CHUNK), lambda i: (0, 0, 0)),
            pl.BlockSpec((CHUNK, D_A), const),
            pl.BlockSpec((1, D_A), const),
        ],
        out_specs=[
            pl.BlockSpec((TM_TOK, D_A), lambda i: (i, 0)),
            pl.BlockSpec((TM_TOK, 3 * D_B), lambda i: (i, 0)),
        ],
        out_shape=[jax.ShapeDtypeStruct((n, D_A), BF16), jax.ShapeDtypeStruct((n, 3 * D_B), BF16)],
        scratch_shapes=[pltpu.VMEM((TM_TOK, D_A), F32)],
        compiler_params=pltpu.CompilerParams(dimension_semantics=("arbitrary",), vmem_limit_bytes=VMEM_LIMIT),
        name="in_proj",
    )(x, nm, win, gv, wsp, bsb, ona)


def _natten_kernel(rows, q_ref, kp_ref, kc_ref, kn_ref, vp_ref, vc_ref, vn_ref, bias_ref, onb_ref, o_ref,
                   kwin, vlo, vhi, obuf):
    i = pl.program_id(1)
    lane_w = lax.broadcasted_iota(jnp.int32, (TQ, D_B), 1)
    lo_w = (lane_w % 128) < HEAD_DIM
    zero_w = jnp.zeros((TQ, D_B), BF16)
    for b, (k_ref, v_ref) in enumerate(((kp_ref, vp_ref), (kc_ref, vc_ref), (kn_ref, vn_ref))):
        kwin[b * TQ:(b + 1) * TQ, :] = k_ref[...]
        v = v_ref[...]
        vlo[b * TQ:(b + 1) * TQ, :] = jnp.where(lo_w, v, zero_w)
        vhi[b * TQ:(b + 1) * TQ, :] = jnp.where(lo_w, zero_w, v)
    lane = lax.broadcasted_iota(jnp.int32, (GRID_W, 128), 1)
    is_lo = lane < HEAD_DIM
    zero = jnp.zeros((GRID_W, 128), BF16)
    nkey = WIN_H * GRID_W
    for rr in range(ROWS_BLK):
        r = i * ROWS_BLK + rr
        rs = jnp.clip(r - WIN_H // 2, 0, rows - WIN_H)
        d = r - rs
        start = pl.multiple_of((rs - (i - 1) * ROWS_BLK) * GRID_W, GRID_W)
        for j in range(H_B // 2):
            cs = slice(j * 128, (j + 1) * 128)
            q2 = q_ref[rr * GRID_W:(rr + 1) * GRID_W, cs]
            k2 = kwin[pl.ds(start, nkey), cs]
            s0 = _dot_nt(jnp.where(is_lo, q2, zero), k2) + bias_ref[d, 2 * j]
            s1 = _dot_nt(jnp.where(is_lo, zero, q2), k2) + bias_ref[d, 2 * j + 1]
            p0 = jnp.exp(s0 - jnp.max(s0, axis=-1, keepdims=True))
            p1 = jnp.exp(s1 - jnp.max(s1, axis=-1, keepdims=True))
            l0 = jnp.sum(p0, axis=-1, keepdims=True)
            l1 = jnp.sum(p1, axis=-1, keepdims=True)
            pcat = jnp.concatenate([p0.astype(BF16), p1.astype(BF16)], axis=1)
            vcat = jnp.concatenate([vlo[pl.ds(start, nkey), cs], vhi[pl.ds(start, nkey), cs]], axis=0)
            o = _dot(pcat, vcat)
            obuf[rr * GRID_W:(rr + 1) * GRID_W, cs] = o * jnp.where(is_lo, 1.0 / l0, 1.0 / l1)
    o_ref[...] = _rms(obuf[...], onb_ref[...]).astype(BF16)


def _natten(qkv, bias, onb, n_seq, seq_len):
    rows = seq_len // GRID_W
    nblk = rows // ROWS_BLK
    blk = lambda col, f: pl.BlockSpec((TQ, D_B), lambda b, i: (b * nblk + f(i), col))
    prev = lambda i: jnp.maximum(i - 1, 0)
    cur = lambda i: i
    nxt = lambda i: jnp.minimum(i + 1, nblk - 1)
    return pl.pallas_call(
        functools.partial(_natten_kernel, rows),
        grid=(n_seq, nblk),
        in_specs=[
            blk(0, cur),
            blk(1, prev), blk(1, cur), blk(1, nxt),
            blk(2, prev), blk(2, cur), blk(2, nxt),
            pl.BlockSpec(memory_space=pltpu.VMEM),
            pl.BlockSpec((1, D_B), lambda b, i: (0, 0)),
        ],
        out_specs=pl.BlockSpec((TQ, D_B), lambda b, i: (b * nblk + i, 0)),
        out_shape=jax.ShapeDtypeStruct((n_seq * seq_len, D_B), BF16),
        scratch_shapes=[
            pltpu.VMEM((3 * TQ, D_B), BF16),
            pltpu.VMEM((3 * TQ, D_B), BF16),
            pltpu.VMEM((3 * TQ, D_B), BF16),
            pltpu.VMEM((TQ, D_B), F32),
        ],
        compiler_params=pltpu.CompilerParams(dimension_semantics=("arbitrary", "arbitrary"),
                                             vmem_limit_bytes=VMEM_LIMIT),
        name="natten",
    )(qkv, qkv, qkv, qkv, qkv, qkv, qkv, bias, onb)


def _natten_bias(rpb):
    c = np.arange(GRID_W)
    cs = np.clip(c - WIN_W // 2, 0, GRID_W - WIN_W)
    col_mask = (c[None, :] >= cs[:, None]) & (c[None, :] < cs[:, None] + WIN_W)
    col_rel = np.clip(c[None, :] - c[:, None], -(WIN_W - 1), WIN_W - 1) + (WIN_W - 1)
    d = np.arange(WIN_H)
    row_rel = np.arange(WIN_H)[None, :] - d[:, None] + (WIN_H - 1)
    b = rpb[:, row_rel[:, None, :, None], col_rel[None, :, None, :]].astype(F32)
    b = jnp.where(col_mask[None, None, :, None, :], b, NEG)
    return jnp.transpose(b, (1, 0, 2, 3, 4)).reshape(WIN_H, H_B, GRID_W, WIN_H * GRID_W)


def _out_proj_kernel(ya_ref, yb_ref, xr_ref, woa_ref, wob_ref, nf_ref, rwt_ref, rb_ref,
                     x2_ref, ids_ref, gates_ref):
    y = _dot(ya_ref[...], woa_ref[...]) + _dot(yb_ref[...], wob_ref[...])
    x2 = xr_ref[...] + y
    x2_ref[...] = x2
    xn = _rms(x2, nf_ref[...])
    lt = _dot_nt(rwt_ref[...], xn.astype(BF16)) + rb_ref[...]
    pj = [lt[8 * j:8 * (j + 1), :] for j in range(E_PER_GROUP)]
    m8 = jnp.maximum(jnp.maximum(pj[0], pj[1]), jnp.maximum(pj[2], pj[3]))
    m = jnp.max(m8, axis=0, keepdims=True)
    e = [jnp.exp(p - m) for p in pj]
    gs = None
    for a, b in PAIRS:
        s = e[a] + e[b]
        gs = s if gs is None else jnp.maximum(gs, s)
    best = gs[0:1, :]
    gsel = jnp.zeros(best.shape, jnp.int32)
    for g in range(1, N_GROUPS):
        cand = gs[g:g + 1, :]
        upd = cand > best
        gsel = jnp.where(upd, g, gsel)
        best = jnp.where(upd, cand, best)
    es = []
    for j in range(E_PER_GROUP):
        v = e[j][0:1, :]
        for g in range(1, N_GROUPS):
            v = jnp.where(gsel == g, e[j][g:g + 1, :], v)
        es.append(v)
    v1 = es[0]
    i1 = jnp.zeros(v1.shape, jnp.int32)
    for j in range(1, E_PER_GROUP):
        upd = es[j] > v1
        i1 = jnp.where(upd, j, i1)
        v1 = jnp.where(upd, es[j], v1)
    v2 = jnp.full(v1.shape, -1.0, F32)
    i2 = jnp.zeros(v1.shape, jnp.int32)
    for j in range(E_PER_GROUP):
        upd = (i1 != j) & (es[j] > v2)
        i2 = jnp.where(upd, j, i2)
        v2 = jnp.where(upd, es[j], v2)
    tot = v1 + v2
    g1 = v1 / tot
    g2 = v2 / tot
    first_lo = i1 < i2
    a = jnp.where(first_lo, i1, i2)
    b = jnp.where(first_lo, i2, i1)
    ga = jnp.where(first_lo, g1, g2)
    gb = jnp.where(first_lo, g2, g1)
    base = jnp.where(a == 0, 0, jnp.where(a == 1, 3, 5))
    cls = gsel * N_PAIRS + base + (b - a - 1)
    zi = jnp.zeros((5,) + cls.shape[1:], jnp.int32)
    ids_ref[...] = jnp.concatenate([cls, gsel * E_PER_GROUP + a, gsel * E_PER_GROUP + b, zi], axis=0)
    zf = jnp.zeros((6,) + ga.shape[1:], F32)
    gates_ref[...] = jnp.concatenate([ga, gb, zf], axis=0)


def _out_proj(ya, yb, xr, woa, wob, nf, rwt, rb):
    n = ya.shape[0]
    const = lambda i: (0, 0)
    return pl.pallas_call(
        _out_proj_kernel,
        grid=(n // TM_TOK,),
        in_specs=[
            pl.BlockSpec((TM_TOK, D_A), lambda i: (i, 0)),
            pl.BlockSpec((TM_TOK, D_B), lambda i: (i, 0)),
            pl.BlockSpec((TM_TOK, D_MODEL), lambda i: (i, 0)),
            pl.BlockSpec((D_A, D_MODEL), const),
            pl.BlockSpec((D_B, D_MODEL), const),
            pl.BlockSpec((1, D_MODEL), const),
            pl.BlockSpec((8 * E_PER_GROUP, D_MODEL), const),
            pl.BlockSpec((8 * E_PER_GROUP, 1), const),
        ],
        out_specs=[
            pl.BlockSpec((TM_TOK, D_MODEL), lambda i: (i, 0)),
            pl.BlockSpec((8, TM_TOK), lambda i: (0, i)),
            pl.BlockSpec((8, TM_TOK), lambda i: (0, i)),
        ],
        out_shape=[
            jax.ShapeDtypeStruct((n, D_MODEL), F32),
            jax.ShapeDtypeStruct((8, n), jnp.int32),
            jax.ShapeDtypeStruct((8, n), F32),
        ],
        compiler_params=pltpu.CompilerParams(dimension_semantics=("arbitrary",), vmem_limit_bytes=VMEM_LIMIT),
        name="out_proj",
    )(ya, yb, xr, woa, wob, nf, rwt, rb)


def _moe_kernel(ea_ref, eb_ref, src_ref, srcn_ref, dst_ref, gate_ref, nf_ref,
                wga_ref, wua_ref, wda_ref, wgb_ref, wub_ref, wdb_ref, x2_hbm, x3_hbm,
                xbuf, obuf, gsem, ssem):
    j = pl.program_id(0)
    n = pl.num_programs(0)
    slot = j % 2

    def gather_rows(idx_ref, s):
        def body(r, c):
            pltpu.make_async_copy(x2_hbm.at[pl.ds(idx_ref[r], 1)], xbuf.at[s, pl.ds(r, 1)], gsem.at[s]).start()
            return c
        lax.fori_loop(0, TM_MOE, body, 0, unroll=8)

    def wait_gather(s):
        pltpu.make_async_copy(x2_hbm.at[pl.ds(0, TM_MOE)], xbuf.at[s], gsem.at[s]).wait()

    def scatter_rows(s):
        def body(r, c):
            pltpu.make_async_copy(obuf.at[s, pl.ds(r, 1)], x3_hbm.at[pl.ds(dst_ref[r], 1)], ssem.at[s]).start()
            return c
        lax.fori_loop(0, TM_MOE, body, 0, unroll=8)

    def wait_scatter(s):
        pltpu.make_async_copy(obuf.at[s], x3_hbm.at[pl.ds(0, TM_MOE)], ssem.at[s]).wait()

    @pl.when(j == 0)
    def _():
        gather_rows(src_ref, 0)

    wait_gather(slot)

    @pl.when(j + 1 < n)
    def _():
        gather_rows(srcn_ref, 1 - slot)

    @pl.when(j >= 2)
    def _():
        wait_scatter(slot)

    x = xbuf[slot]
    xn = _rms(x, nf_ref[...]).astype(BF16)
    ha = jax.nn.silu(_dot(xn, wga_ref[...])) * _dot(xn, wua_ref[...])
    oa = _dot(ha.astype(BF16), wda_ref[...])
    hb = jax.nn.silu(_dot(xn, wgb_ref[...])) * _dot(xn, wub_ref[...])
    ob = _dot(hb.astype(BF16), wdb_ref[...])
    obuf[slot] = x + gate_ref[:, 0:1] * oa + gate_ref[:, 1:2] * ob
    scatter_rows(slot)

    @pl.when(j == n - 1)
    def _():
        wait_scatter(slot)

        @pl.when(n >= 2)
        def _():
            wait_scatter(1 - slot)


def _moe(x2, tile_ea, tile_eb, src, dst, gates_sorted, nf, wg, wu, wd, n_rows_out):
    n_tiles = src.shape[0] // TM_MOE
    smem_blk = lambda f: pl.BlockSpec((TM_MOE,), lambda j, ea, eb: (f(j),), memory_space=pltpu.SMEM)
    wspec = lambda shape, which: pl.BlockSpec(
        (None,) + shape, lambda j, ea, eb: ((ea, eb)[which][j], 0, 0))
    grid_spec = pltpu.PrefetchScalarGridSpec(
        num_scalar_prefetch=2,
        grid=(n_tiles,),
        in_specs=[
            smem_blk(lambda j: j),
            smem_blk(lambda j: jnp.minimum(j + 1, n_tiles - 1)),
            smem_blk(lambda j: j),
            pl.BlockSpec((TM_MOE, 2), lambda j, ea, eb: (j, 0)),
            pl.BlockSpec((1, D_MODEL), lambda j, ea, eb: (0, 0)),
            wspec((D_MODEL, D_EXPERT), 0), wspec((D_MODEL, D_EXPERT), 0), wspec((D_EXPERT, D_MODEL), 0),
            wspec((D_MODEL, D_EXPERT), 1), wspec((D_MODEL, D_EXPERT), 1), wspec((D_EXPERT, D_MODEL), 1),
            pl.BlockSpec(memory_space=pl.ANY),
        ],
        out_specs=pl.BlockSpec(memory_space=pl.ANY),
        scratch_shapes=[
            pltpu.VMEM((2, TM_MOE, D_MODEL), F32),
            pltpu.VMEM((2, TM_MOE, D_MODEL), F32),
            pltpu.SemaphoreType.DMA((2,)),
            pltpu.SemaphoreType.DMA((2,)),
        ],
    )
    return pl.pallas_call(
        _moe_kernel,
        grid_spec=grid_spec,
        out_shape=jax.ShapeDtypeStruct((n_rows_out, D_MODEL), F32),
        compiler_params=pltpu.CompilerParams(dimension_semantics=("arbitrary",), vmem_limit_bytes=VMEM_LIMIT),
        name="moe",
    )(tile_ea, tile_eb, src, src, dst, gates_sorted, nf, wg, wu, wd, wg, wu, wd, x2)


def _moe_plan(ids, gates, n):
    cls = ids[0]
    n_tiles = n // TM_MOE + N_CLASSES
    p_rows = n_tiles * TM_MOE
    order = jnp.argsort(cls, stable=True).astype(jnp.int32)
    counts = jnp.sum((cls[None, :] == jnp.arange(N_CLASSES, dtype=jnp.int32)[:, None]).astype(jnp.int32), axis=1)
    tiles_per = (counts + TM_MOE - 1) // TM_MOE
    tile_end = jnp.cumsum(tiles_per)
    pad_start = (tile_end - tiles_per) * TM_MOE
    sorted_start = jnp.cumsum(counts) - counts
    tile_cls = jnp.minimum(jnp.searchsorted(tile_end, jnp.arange(n_tiles, dtype=jnp.int32), side="right"),
                           N_CLASSES - 1).astype(jnp.int32)
    pair_a = jnp.array([p[0] for p in PAIRS], jnp.int32)
    pair_b = jnp.array([p[1] for p in PAIRS], jnp.int32)
    tile_ea = (tile_cls // N_PAIRS) * E_PER_GROUP + pair_a[tile_cls % N_PAIRS]
    tile_eb = (tile_cls // N_PAIRS) * E_PER_GROUP + pair_b[tile_cls % N_PAIRS]
    row_cls = jnp.repeat(tile_cls, TM_MOE)
    p = jnp.arange(p_rows, dtype=jnp.int32)
    k = p - pad_start[row_cls]
    valid = (k < counts[row_cls]) & (p < tile_end[-1] * TM_MOE)
    tok = order[jnp.clip(sorted_start[row_cls] + k, 0, n - 1)]
    src = jnp.where(valid, tok, 0)
    pad_rank = jnp.cumsum((~valid).astype(jnp.int32)) - 1
    dst = jnp.where(valid, tok, n + pad_rank)
    g = jnp.where(valid[:, None], jnp.transpose(gates[0:2])[src], 0.0)
    return tile_ea.astype(jnp.int32), tile_eb.astype(jnp.int32), src, dst, g, p_rows - n


def _final_norm_kernel(x_ref, g_ref, o_ref):
    o_ref[...] = _rms(x_ref[...], g_ref[...])


def _final_norm(x, g, row0, n_rows):
    blk0 = row0 // TM_TOK
    return pl.pallas_call(
        _final_norm_kernel,
        grid=(n_rows // TM_TOK,),
        in_specs=[pl.BlockSpec((TM_TOK, D_MODEL), lambda i: (blk0 + i, 0)),
                  pl.BlockSpec((1, D_MODEL), lambda i: (0, 0))],
        out_specs=pl.BlockSpec((TM_TOK, D_MODEL), lambda i: (i, 0)),
        out_shape=jax.ShapeDtypeStruct((n_rows, D_MODEL), F32),
        compiler_params=pltpu.CompilerParams(dimension_semantics=("arbitrary",), vmem_limit_bytes=VMEM_LIMIT),
        name="final_norm",
    )(x, g)


def _trunk(x, n_seq, seq_len, norm_mix, w_in, gmlp_v_norm, gmlp_ws, gmlp_bs, na_rpb, out_norm_a, out_norm_b,
           w_out, norm_ffn, router_w, router_bias, w_gate, w_up, w_down):
    n = n_seq * seq_len
    row = lambda v: v.reshape(1, -1).astype(F32)
    perm = np.array([[E_PER_GROUP * g + j for g in range(N_GROUPS)] for j in range(E_PER_GROUP)])
    rwt = jnp.zeros((E_PER_GROUP, 8, D_MODEL), F32).at[:, :N_GROUPS, :].set(jnp.transpose(router_w)[perm])
    rwt = rwt.reshape(8 * E_PER_GROUP, D_MODEL).astype(BF16)
    rb = jnp.full((E_PER_GROUP, 8), NEG, F32).at[:, :N_GROUPS].set(router_bias.astype(F32)[perm])
    rb = rb.reshape(8 * E_PER_GROUP, 1)
    for l in range(DEPTH):
        ws = gmlp_ws[l].astype(BF16)
        wsp = jnp.concatenate([ws[0::2], ws[1::2]], axis=2)
        bsb = jnp.repeat(jnp.transpose(gmlp_bs[l]).astype(F32), HEAD_DIM, axis=1)
        ya, qkv = _in_proj(x, n, row(norm_mix[l]), w_in[l].astype(BF16), row(gmlp_v_norm[l]), wsp, bsb,
                           row(out_norm_a[l]))
        yb = _natten(qkv, _natten_bias(na_rpb[l]), row(out_norm_b[l]), n_seq, seq_len)
        wo = w_out[l].astype(BF16)
        x2, ids, gates = _out_proj(ya, yb, x, wo[:D_A], wo[D_A:], row(norm_ffn[l]), rwt, rb)
        tile_ea, tile_eb, src, dst, g_sorted, n_pad = _moe_plan(ids, gates, n)
        x = _moe(x2, tile_ea, tile_eb, src, dst, g_sorted, row(norm_ffn[l]), w_gate[l].astype(BF16),
                 w_up[l].astype(BF16), w_down[l].astype(BF16), n + n_pad)
    return x


def kernel(x_prompt, x_sample, norm_mix, w_in, gmlp_v_norm, gmlp_ws, gmlp_bs, na_rpb, out_norm_a, out_norm_b,
           w_out, norm_ffn, router_w, router_bias, w_gate, w_up, w_down, norm_final):
    bp, sp, _ = x_prompt.shape
    bs, ss, _ = x_sample.shape
    assert sp == ss
    x = jnp.concatenate([x_prompt.reshape(bp * sp, D_MODEL), x_sample.reshape(bs * ss, D_MODEL)], axis=0)
    x = _trunk(x, bp + bs, sp, norm_mix, w_in, gmlp_v_norm, gmlp_ws, gmlp_bs, na_rpb, out_norm_a, out_norm_b,
               w_out, norm_ffn, router_w, router_bias, w_gate, w_up, w_down)
    g = norm_final.reshape(1, -1).astype(F32)
    y_prompt = _final_norm(x, g, 0, bp * sp).reshape(bp, sp, D_MODEL)
    y_sample = _final_norm(x, g, bp * sp, bs * ss).reshape(bs, ss, D_MODEL)
    return (y_prompt, y_sample)
```

```python
import functools

import jax
import jax.numpy as jnp
import numpy as np
from jax import lax
from jax.experimental import pallas as pl
from jax.experimental.pallas import tpu as pltpu

F32 = jnp.float32
BF16 = jnp.bfloat16

D_MODEL = 1024
DEPTH = 2
D_A = 512
D_B = 512
HEAD_DIM = 64
H_A = D_A // HEAD_DIM
H_B = D_B // HEAD_DIM
CHUNK = 128
GRID_W = 64
WIN_H = 8
WIN_W = 16
N_EXPERTS = 16
N_GROUPS = 4
E_PER_GROUP = 4
D_EXPERT = 512
D_IN = 2 * D_A + 3 * D_B
EPS = 1e-6
NEG = -1e30

PAIRS = ((0, 1), (0, 2), (0, 3), (1, 2), (1, 3), (2, 3))
N_PAIRS = len(PAIRS)
N_CLASSES = N_GROUPS * N_PAIRS

SUBLANES = 8
LANES = 128
TM_TOK = 512
ROWS_BLK = 8
TQ = ROWS_BLK * GRID_W
TM_MOE = 256
VMEM_LIMIT = 56 * 1024 * 1024


def _rms(x, g):
    return x * lax.rsqrt(jnp.mean(x * x, axis=-1, keepdims=True) + EPS) * g


def _dot(a, b):
    return jnp.dot(a, b, preferred_element_type=F32)


def _dot_nt(a, b):
    return lax.dot_general(a, b, (((1,), (1,)), ((), ())), preferred_element_type=F32)


def _two_source_specs(shape, n_a):
    return [pl.BlockSpec(shape, lambda i: (jnp.minimum(i, n_a - 1), 0)),
            pl.BlockSpec(shape, lambda i: (jnp.maximum(i - n_a, 0), 0))]


def _load_tiled(ref, tm):
    return jnp.concatenate([ref[pl.ds(s, tm, stride=SUBLANES), :] for s in range(SUBLANES)], axis=1)


def _store_tiled(ref, val, tm):
    for s in range(SUBLANES):
        ref[pl.ds(s, tm, stride=SUBLANES), :] = val[:, s * LANES:(s + 1) * LANES]


def _token_specs(xs, first_block=0):
    if len(xs) == 1:
        return None, [pl.BlockSpec((TM_TOK * SUBLANES, LANES), lambda i: (first_block + i, 0))]
    n_a = xs[0].shape[0] // TM_TOK
    return n_a, _two_source_specs((TM_TOK, D_MODEL), n_a)


def _load_tokens(n_a, refs):
    if n_a is None:
        return _load_tiled(refs[0], TM_TOK), refs[1:]
    return jnp.where(pl.program_id(0) < n_a, refs[0][...], refs[1][...]), refs[2:]


def _in_proj_kernel(n_a, *refs):
    x, refs = _load_tokens(n_a, refs)
    nm_ref, win_ref, gv_ref, wsp_ref, bsb_ref, ona_ref, ya_ref, qkv_ref, ybuf = refs
    xn = _rms(x, nm_ref[...])
    z = _dot(xn.astype(BF16), win_ref[...])
    qkv_ref[:, 0:D_B] = (z[:, 2 * D_A:2 * D_A + D_B] * (HEAD_DIM ** -0.5)).astype(BF16)
    qkv_ref[:, D_B:3 * D_B] = z[:, 2 * D_A + D_B:].astype(BF16)
    za = jax.nn.gelu(z[:, :2 * D_A])
    u = za[:, :D_A]
    vn = _rms(za[:, D_A:], gv_ref[...]).astype(BF16)
    lane = lax.broadcasted_iota(jnp.int32, (CHUNK, 2 * HEAD_DIM), 1)
    is_lo = lane < HEAD_DIM
    zero = jnp.zeros((CHUNK, 2 * HEAD_DIM), BF16)
    tm = x.shape[0]
    for c2 in range(tm // (2 * CHUNK)):
        for j in range(H_A // 2):
            cols = []
            for cc in range(2):
                r0 = (2 * c2 + cc) * CHUNK
                vp = vn[r0:r0 + CHUNK, j * 128:(j + 1) * 128]
                cols.append(jnp.concatenate([jnp.where(is_lo, vp, zero), jnp.where(is_lo, zero, vp)], axis=0))
            rhs = jnp.concatenate(cols, axis=1)
            sv = _dot(wsp_ref[j], rhs)
            for cc in range(2):
                r0 = (2 * c2 + cc) * CHUNK
                ybuf[r0:r0 + CHUNK, j * 128:(j + 1) * 128] = u[r0:r0 + CHUNK, j * 128:(j + 1) * 128] * (
                    sv[:, cc * 128:(cc + 1) * 128] + bsb_ref[:, j * 128:(j + 1) * 128])
    ya_ref[...] = _rms(ybuf[...], ona_ref[...]).astype(BF16)


def _in_proj(xs, n, nm, win, gv, wsp, bsb, ona):
    const = lambda i: (0, 0)
    n_a, x_specs = _token_specs(xs)
    return pl.pallas_call(
        functools.partial(_in_proj_kernel, n_a),
        grid=(n // TM_TOK,),
        in_specs=x_specs + [
            pl.BlockSpec((1, D_MODEL), const),
            pl.BlockSpec((D_MODEL, D_IN), const),
            pl.BlockSpec((1, D_A), const),
            pl.BlockSpec((H_A // 2, CHUNK, 2 * CHUNK), lambda i: (0, 0, 0)),
            pl.BlockSpec((CHUNK, D_A), const),
            pl.BlockSpec((1, D_A), const),
        ],
        out_specs=[
            pl.BlockSpec((TM_TOK, D_A), lambda i: (i, 0)),
            pl.BlockSpec((TM_TOK, 3 * D_B), lambda i: (i, 0)),
        ],
        out_shape=[jax.ShapeDtypeStruct((n, D_A), BF16), jax.ShapeDtypeStruct((n, 3 * D_B), BF16)],
        scratch_shapes=[pltpu.VMEM((TM_TOK, D_A), F32)],
        compiler_params=pltpu.CompilerParams(dimension_semantics=("arbitrary",), vmem_limit_bytes=VMEM_LIMIT),
        name="in_proj",
    )(*xs, nm, win, gv, wsp, bsb, ona)


def _natten_kernel(rows, q_ref, kp_ref, kc_ref, kn_ref, vp_ref, vc_ref, vn_ref, bias_ref, onb_ref, o_ref,
                   kwin, vwin, obuf):
    i = pl.program_id(1)
    for b, (k_ref, v_ref) in enumerate(((kp_ref, vp_ref), (kc_ref, vc_ref), (kn_ref, vn_ref))):
        kwin[b * TQ:(b + 1) * TQ, :] = k_ref[...]
        vwin[b * TQ:(b + 1) * TQ, :] = v_ref[...]
    lane = lax.broadcasted_iota(jnp.int32, (GRID_W, 128), 1)
    is_lo = lane < HEAD_DIM
    zero = jnp.zeros((GRID_W, 128), BF16)
    nkey = WIN_H * GRID_W
    for rr in range(ROWS_BLK):
        r = i * ROWS_BLK + rr
        rs = jnp.clip(r - WIN_H // 2, 0, rows - WIN_H)
        d = r - rs
        start = pl.multiple_of((rs - (i - 1) * ROWS_BLK) * GRID_W, GRID_W)
        for j in range(H_B // 2):
            cs = slice(j * 128, (j + 1) * 128)
            q2 = q_ref[rr * GRID_W:(rr + 1) * GRID_W, cs]
            qs = jnp.concatenate([jnp.where(is_lo, q2, zero), jnp.where(is_lo, zero, q2)], axis=0)
            s = _dot_nt(qs, kwin[pl.ds(start, nkey), cs]) + bias_ref[d, j]
            p = jnp.exp(s - jnp.max(s, axis=-1, keepdims=True))
            l = jnp.sum(p, axis=-1, keepdims=True)
            o = _dot(p.astype(BF16), vwin[pl.ds(start, nkey), cs]) * (1.0 / l)
            obuf[rr * GRID_W:(rr + 1) * GRID_W, cs] = jnp.where(is_lo, o[:GRID_W], o[GRID_W:])
    o_ref[...] = _rms(obuf[...], onb_ref[...]).astype(BF16)


def _natten(qkv, bias, onb, n_seq, seq_len):
    rows = seq_len // GRID_W
    nblk = rows // ROWS_BLK
    blk = lambda col, f: pl.BlockSpec((TQ, D_B), lambda b, i: (b * nblk + f(i), col))
    prev = lambda i: jnp.maximum(i - 1, 0)
    cur = lambda i: i
    nxt = lambda i: jnp.minimum(i + 1, nblk - 1)
    return pl.pallas_call(
        functools.partial(_natten_kernel, rows),
        grid=(n_seq, nblk),
        in_specs=[
            blk(0, cur),
            blk(1, prev), blk(1, cur), blk(1, nxt),
            blk(2, prev), blk(2, cur), blk(2, nxt),
            pl.BlockSpec(memory_space=pltpu.VMEM),
            pl.BlockSpec((1, D_B), lambda b, i: (0, 0)),
        ],
        out_specs=pl.BlockSpec((TQ, D_B), lambda b, i: (b * nblk + i, 0)),
        out_shape=jax.ShapeDtypeStruct((n_seq * seq_len, D_B), BF16),
        scratch_shapes=[
            pltpu.VMEM((3 * TQ, D_B), BF16),
            pltpu.VMEM((3 * TQ, D_B), BF16),
            pltpu.VMEM((TQ, D_B), F32),
        ],
        compiler_params=pltpu.CompilerParams(dimension_semantics=("arbitrary", "arbitrary"),
                                             vmem_limit_bytes=VMEM_LIMIT),
        name="natten",
    )(qkv, qkv, qkv, qkv, qkv, qkv, qkv, bias, onb)


def _natten_bias(rpb):
    c = np.arange(GRID_W)
    cs = np.clip(c - WIN_W // 2, 0, GRID_W - WIN_W)
    col_mask = (c[None, :] >= cs[:, None]) & (c[None, :] < cs[:, None] + WIN_W)
    col_rel = np.clip(c[None, :] - c[:, None], -(WIN_W - 1), WIN_W - 1) + (WIN_W - 1)
    d = np.arange(WIN_H)
    row_rel = np.arange(WIN_H)[None, :] - d[:, None] + (WIN_H - 1)
    row_sel = np.eye(2 * WIN_H - 1, dtype=np.float32)[row_rel]
    col_sel = np.eye(2 * WIN_W - 1, dtype=np.float32)[col_rel]
    b = jnp.einsum("dia,ckb,hab->dhcik", row_sel, col_sel, rpb.astype(F32), precision=lax.Precision.HIGHEST)
    b = jnp.where(col_mask[None, None, :, None, :], b, NEG)
    return b.reshape(WIN_H, H_B // 2, 2 * GRID_W, WIN_H * GRID_W)


def _out_proj_kernel(n_a, ya_ref, yb_ref, *refs):
    xr, refs = _load_tokens(n_a, refs)
    woa_ref, wob_ref, nf_ref, rwt_ref, rb_ref, x2_ref, ids_ref = refs
    y = _dot(ya_ref[...], woa_ref[...]) + _dot(yb_ref[...], wob_ref[...])
    x2 = xr + y
    _store_tiled(x2_ref, x2, TM_TOK)
    xn = _rms(x2, nf_ref[...])
    lt = _dot_nt(rwt_ref[...], xn.astype(BF16)) + rb_ref[...]
    pj = [lt[8 * j:8 * (j + 1), :] for j in range(E_PER_GROUP)]
    m8 = jnp.maximum(jnp.maximum(pj[0], pj[1]), jnp.maximum(pj[2], pj[3]))
    m = jnp.max(m8, axis=0, keepdims=True)
    e = [jnp.exp(p - m) for p in pj]
    gs = None
    for a, b in PAIRS:
        s = e[a] + e[b]
        gs = s if gs is None else jnp.maximum(gs, s)
    best = gs[0:1, :]
    gsel = jnp.zeros(best.shape, jnp.int32)
    for g in range(1, N_GROUPS):
        cand = gs[g:g + 1, :]
        upd = cand > best
        gsel = jnp.where(upd, g, gsel)
        best = jnp.where(upd, cand, best)
    es = []
    for j in range(E_PER_GROUP):
        v = e[j][0:1, :]
        for g in range(1, N_GROUPS):
            v = jnp.where(gsel == g, e[j][g:g + 1, :], v)
        es.append(v)
    v1 = es[0]
    i1 = jnp.zeros(v1.shape, jnp.int32)
    for j in range(1, E_PER_GROUP):
        upd = es[j] > v1
        i1 = jnp.where(upd, j, i1)
        v1 = jnp.where(upd, es[j], v1)
    v2 = jnp.full(v1.shape, -1.0, F32)
    i2 = jnp.zeros(v1.shape, jnp.int32)
    for j in range(E_PER_GROUP):
        upd = (i1 != j) & (es[j] > v2)
        i2 = jnp.where(upd, j, i2)
        v2 = jnp.where(upd, es[j], v2)
    a = jnp.minimum(i1, i2)
    b = jnp.maximum(i1, i2)
    base = jnp.where(a == 0, 0, jnp.where(a == 1, 3, 5))
    cls = gsel * N_PAIRS + base + (b - a - 1)
    ids_ref[...] = jnp.concatenate([cls, jnp.zeros((SUBLANES - 1,) + cls.shape[1:], jnp.int32)], axis=0)


def _out_proj(ya, yb, xrs, woa, wob, nf, rwt, rb):
    n = ya.shape[0]
    const = lambda i: (0, 0)
    n_a, x_specs = _token_specs(xrs)
    return pl.pallas_call(
        functools.partial(_out_proj_kernel, n_a),
        grid=(n // TM_TOK,),
        in_specs=[
            pl.BlockSpec((TM_TOK, D_A), lambda i: (i, 0)),
            pl.BlockSpec((TM_TOK, D_B), lambda i: (i, 0)),
        ] + x_specs + [
            pl.BlockSpec((D_A, D_MODEL), const),
            pl.BlockSpec((D_B, D_MODEL), const),
            pl.BlockSpec((1, D_MODEL), const),
            pl.BlockSpec((8 * E_PER_GROUP, D_MODEL), const),
            pl.BlockSpec((8 * E_PER_GROUP, 1), const),
        ],
        out_specs=[
            pl.BlockSpec((TM_TOK * SUBLANES, LANES), lambda i: (i, 0)),
            pl.BlockSpec((SUBLANES, TM_TOK), lambda i: (0, i)),
        ],
        out_shape=[
            jax.ShapeDtypeStruct((n * SUBLANES, LANES), F32),
            jax.ShapeDtypeStruct((SUBLANES, n), jnp.int32),
        ],
        compiler_params=pltpu.CompilerParams(dimension_semantics=("arbitrary",), vmem_limit_bytes=VMEM_LIMIT),
        name="out_proj",
    )(ya, yb, *xrs, woa, wob, nf, rwt, rb)


def _moe_kernel(n_tok, ea_ref, eb_ref, row0_ref, cnt_ref, order_ref, nf_ref, rwa_ref, rwb_ref, rba_ref, rbb_ref,
                wga_ref, wua_ref, wda_ref, wgb_ref, wub_ref, wdb_ref, x2_hbm, x3_hbm,
                xbuf, obuf, gsem, ssem):
    j = pl.program_id(0)
    n = pl.num_programs(0)
    slot = j % 2
    unroll = 8

    def tile_of(tok):
        return pl.ds(pl.multiple_of(tok * SUBLANES, SUBLANES), SUBLANES)

    def gather_rows(t, s):
        base = row0_ref[t]

        def body(i, c):
            for k in range(unroll):
                r = i * unroll + k
                pltpu.make_async_copy(x2_hbm.at[tile_of(order_ref[base + r])], xbuf.at[s, tile_of(r)],
                                      gsem.at[s]).start()
            return c
        lax.fori_loop(0, TM_MOE // unroll, body, 0)

    def wait_gather(s):
        pltpu.make_async_copy(x2_hbm.at[pl.ds(0, TM_MOE * SUBLANES)], xbuf.at[s], gsem.at[s]).wait()

    def scatter_rows(t, s):
        base = row0_ref[t]
        cnt = cnt_ref[t]
        pad0 = n_tok + t * TM_MOE - base - cnt

        def body(i, c):
            for k in range(unroll):
                r = i * unroll + k
                dst = jnp.where(r < cnt, order_ref[base + r], pad0 + r)
                pltpu.make_async_copy(obuf.at[s, tile_of(r)], x3_hbm.at[tile_of(dst)], ssem.at[s]).start()
            return c
        lax.fori_loop(0, TM_MOE // unroll, body, 0)

    def wait_scatter(s):
        pltpu.make_async_copy(obuf.at[s], x3_hbm.at[pl.ds(0, TM_MOE * SUBLANES)], ssem.at[s]).wait()

    @pl.when(j == 0)
    def _():
        gather_rows(0, 0)

    wait_gather(slot)

    @pl.when(j + 1 < n)
    def _():
        gather_rows(j + 1, 1 - slot)

    @pl.when(j >= 2)
    def _():
        wait_scatter(slot)

    x = _load_tiled(xbuf.at[slot], TM_MOE)
    xn = _rms(x, nf_ref[...])
    la = jnp.sum(xn * rwa_ref[...], axis=-1, keepdims=True) + rba_ref[:, 0:1]
    lb = jnp.sum(xn * rwb_ref[...], axis=-1, keepdims=True) + rbb_ref[:, 0:1]
    lm = jnp.maximum(la, lb)
    pa = jnp.exp(la - lm)
    pb = jnp.exp(lb - lm)
    tot = pa + pb
    xb = xn.astype(BF16)
    ha = jax.nn.silu(_dot(xb, wga_ref[...])) * _dot(xb, wua_ref[...])
    oa = _dot(ha.astype(BF16), wda_ref[...])
    hb = jax.nn.silu(_dot(xb, wgb_ref[...])) * _dot(xb, wub_ref[...])
    ob = _dot(hb.astype(BF16), wdb_ref[...])
    _store_tiled(obuf.at[slot], x + (pa / tot) * oa + (pb / tot) * ob, TM_MOE)
    scatter_rows(j, slot)

    @pl.when(j == n - 1)
    def _():
        wait_scatter(slot)

        @pl.when(n >= 2)
        def _():
            wait_scatter(1 - slot)


def _moe(x2, plan, nf, rw_rows, rb_rows, wg, wu, wd, n):
    tile_ea, tile_eb, row0, cnt, order = plan
    n_tiles = tile_ea.shape[0]
    n_rows_out = n_tiles * TM_MOE
    wspec = lambda shape, which: pl.BlockSpec(
        (None,) + shape, lambda j, ea, eb, r0, ct, od: ((ea, eb)[which][j], 0, 0))
    grid_spec = pltpu.PrefetchScalarGridSpec(
        num_scalar_prefetch=5,
        grid=(n_tiles,),
        in_specs=[
            pl.BlockSpec((1, D_MODEL), lambda j, *_: (0, 0)),
            wspec((1, D_MODEL), 0), wspec((1, D_MODEL), 1), wspec((1, LANES), 0), wspec((1, LANES), 1),
            wspec((D_MODEL, D_EXPERT), 0), wspec((D_MODEL, D_EXPERT), 0), wspec((D_EXPERT, D_MODEL), 0),
            wspec((D_MODEL, D_EXPERT), 1), wspec((D_MODEL, D_EXPERT), 1), wspec((D_EXPERT, D_MODEL), 1),
            pl.BlockSpec(memory_space=pl.ANY),
        ],
        out_specs=pl.BlockSpec(memory_space=pl.ANY),
        scratch_shapes=[
            pltpu.VMEM((2, TM_MOE * SUBLANES, LANES), F32),
            pltpu.VMEM((2, TM_MOE * SUBLANES, LANES), F32),
            pltpu.SemaphoreType.DMA((2,)),
            pltpu.SemaphoreType.DMA((2,)),
        ],
    )
    return pl.pallas_call(
        functools.partial(_moe_kernel, n),
        grid_spec=grid_spec,
        out_shape=jax.ShapeDtypeStruct((n_rows_out * SUBLANES, LANES), F32),
        compiler_params=pltpu.CompilerParams(dimension_semantics=("arbitrary",), vmem_limit_bytes=VMEM_LIMIT),
        name="moe",
    )(tile_ea, tile_eb, row0, cnt, order, nf, rw_rows, rw_rows, rb_rows, rb_rows, wg, wu, wd, wg, wu, wd, x2)


def _moe_plan(cls, n):
    n_tiles = n // TM_MOE + N_CLASSES
    _, order = lax.sort((cls, lax.iota(jnp.int32, n)), num_keys=1, is_stable=True)
    counts = jnp.sum((cls[None, :] == jnp.arange(N_CLASSES, dtype=jnp.int32)[:, None]).astype(jnp.int32), axis=1)
    tiles_per = (counts + TM_MOE - 1) // TM_MOE
    tile_end = jnp.cumsum(tiles_per)
    sorted_start = jnp.cumsum(counts) - counts
    t = jnp.arange(n_tiles, dtype=jnp.int32)
    tile_cls = jnp.minimum(jnp.sum((tile_end[None, :] <= t[:, None]).astype(jnp.int32), axis=1), N_CLASSES - 1)
    k = t - (tile_end - tiles_per)[tile_cls]
    cnt = jnp.clip(counts[tile_cls] - k * TM_MOE, 0, TM_MOE)
    row0 = jnp.minimum(sorted_start[tile_cls] + k * TM_MOE, n)
    row0 = jnp.where(cnt > 0, row0, jnp.minimum(sorted_start[tile_cls] + counts[tile_cls], n))
    pair_a = jnp.array([p[0] for p in PAIRS], jnp.int32)
    pair_b = jnp.array([p[1] for p in PAIRS], jnp.int32)
    tile_ea = (tile_cls // N_PAIRS) * E_PER_GROUP + pair_a[tile_cls % N_PAIRS]
    tile_eb = (tile_cls // N_PAIRS) * E_PER_GROUP + pair_b[tile_cls % N_PAIRS]
    i32 = lambda v: v.astype(jnp.int32)
    order = jnp.concatenate([order, jnp.zeros((TM_MOE,), jnp.int32)])
    return i32(tile_ea), i32(tile_eb), i32(row0), i32(cnt), i32(order)


def _final_norm_kernel(x_ref, g_ref, o_ref):
    o_ref[...] = _rms(_load_tiled(x_ref, TM_TOK), g_ref[...])


def _final_norm(x, g, row0, n_rows):
    blk0 = row0 // TM_TOK
    return pl.pallas_call(
        _final_norm_kernel,
        grid=(n_rows // TM_TOK,),
        in_specs=[pl.BlockSpec((TM_TOK * SUBLANES, LANES), lambda i: (blk0 + i, 0)),
                  pl.BlockSpec((1, D_MODEL), lambda i: (0, 0))],
        out_specs=pl.BlockSpec((TM_TOK, D_MODEL), lambda i: (i, 0)),
        out_shape=jax.ShapeDtypeStruct((n_rows, D_MODEL), F32),
        compiler_params=pltpu.CompilerParams(dimension_semantics=("arbitrary",), vmem_limit_bytes=VMEM_LIMIT),
        name="final_norm",
    )(x, g)


def _trunk(xs, n_seq, seq_len, norm_mix, w_in, gmlp_v_norm, gmlp_ws, gmlp_bs, na_rpb, out_norm_a, out_norm_b,
           w_out, norm_ffn, router_w, router_bias, w_gate, w_up, w_down):
    n = n_seq * seq_len
    row = lambda v: v.reshape(1, -1).astype(F32)
    perm = np.array([[E_PER_GROUP * g + j for g in range(N_GROUPS)] for j in range(E_PER_GROUP)])
    rwt = jnp.zeros((E_PER_GROUP, 8, D_MODEL), F32).at[:, :N_GROUPS, :].set(jnp.transpose(router_w)[perm])
    rwt = rwt.reshape(8 * E_PER_GROUP, D_MODEL).astype(BF16)
    rb = jnp.full((E_PER_GROUP, 8), NEG, F32).at[:, :N_GROUPS].set(router_bias.astype(F32)[perm])
    rb = rb.reshape(8 * E_PER_GROUP, 1)
    rw_rows = jnp.transpose(router_w).astype(F32).reshape(N_EXPERTS, 1, D_MODEL)
    rb_rows = jnp.broadcast_to(router_bias.astype(F32).reshape(N_EXPERTS, 1, 1), (N_EXPERTS, 1, LANES))
    for l in range(DEPTH):
        ws = gmlp_ws[l].astype(BF16)
        wsp = jnp.concatenate([ws[0::2], ws[1::2]], axis=2)
        bsb = jnp.repeat(jnp.transpose(gmlp_bs[l]).astype(F32), HEAD_DIM, axis=1)
        ya, qkv = _in_proj(xs, n, row(norm_mix[l]), w_in[l].astype(BF16), row(gmlp_v_norm[l]), wsp, bsb,
                           row(out_norm_a[l]))
        yb = _natten(qkv, _natten_bias(na_rpb[l]), row(out_norm_b[l]), n_seq, seq_len)
        wo = w_out[l].astype(BF16)
        x2, ids = _out_proj(ya, yb, xs, wo[:D_A], wo[D_A:], row(norm_ffn[l]), rwt, rb)
        xs = [_moe(x2, _moe_plan(ids[0], n), row(norm_ffn[l]), rw_rows, rb_rows, w_gate[l].astype(BF16),
                   w_up[l].astype(BF16), w_down[l].astype(BF16), n)]
    return xs[0]


def kernel(x_prompt, x_sample, norm_mix, w_in, gmlp_v_norm, gmlp_ws, gmlp_bs, na_rpb, out_norm_a, out_norm_b,
           w_out, norm_ffn, router_w, router_bias, w_gate, w_up, w_down, norm_final):
    bp, sp, _ = x_prompt.shape
    bs, ss, _ = x_sample.shape
    assert sp == ss and (bp * sp) % TM_TOK == 0
    xs = [x_prompt.reshape(bp * sp, D_MODEL), x_sample.reshape(bs * ss, D_MODEL)]
    x = _trunk(xs, bp + bs, sp, norm_mix, w_in, gmlp_v_norm, gmlp_ws, gmlp_bs, na_rpb, out_norm_a, out_norm_b,
               w_out, norm_ffn, router_w, router_bias, w_gate, w_up, w_down)
    g = norm_final.reshape(1, -1).astype(F32)
    y_prompt = _final_norm(x, g, 0, bp * sp).reshape(bp, sp, D_MODEL)
    y_sample = _final_norm(x, g, bp * sp, bs * ss).reshape(bs, ss, D_MODEL)
    return (y_prompt, y_sample)
```

```python
import functools

import jax
import jax.numpy as jnp
import numpy as np
from jax import lax
from jax.experimental import pallas as pl
from jax.experimental.pallas import tpu as pltpu

F32 = jnp.float32
BF16 = jnp.bfloat16

D_MODEL = 1024
DEPTH = 2
D_A = 512
D_B = 512
HEAD_DIM = 64
H_A = D_A // HEAD_DIM
H_B = D_B // HEAD_DIM
CHUNK = 128
GRID_W = 64
WIN_H = 8
WIN_W = 16
N_EXPERTS = 16
N_GROUPS = 4
E_PER_GROUP = 4
D_EXPERT = 512
D_IN = 2 * D_A + 3 * D_B
EPS = 1e-6
NEG = -1e30

PAIRS = ((0, 1), (0, 2), (0, 3), (1, 2), (1, 3), (2, 3))
N_PAIRS = len(PAIRS)
N_CLASSES = N_GROUPS * N_PAIRS

SUBLANES = 8
LANES = 128
TM_TOK = 512
ROWS_BLK = 8
TQ = ROWS_BLK * GRID_W
TM_MOE = 256
VMEM_LIMIT = 56 * 1024 * 1024


def _rms(x, g):
    return x * lax.rsqrt(jnp.mean(x * x, axis=-1, keepdims=True) + EPS) * g


def _dot(a, b):
    return jnp.dot(a, b, preferred_element_type=F32)


def _dot_nt(a, b):
    return lax.dot_general(a, b, (((1,), (1,)), ((), ())), preferred_element_type=F32)


def _two_source_specs(shape, n_a):
    return [pl.BlockSpec(shape, lambda i: (jnp.minimum(i, n_a - 1), 0)),
            pl.BlockSpec(shape, lambda i: (jnp.maximum(i - n_a, 0), 0))]


def _load_tiled(ref, tm):
    return jnp.concatenate([ref[pl.ds(s, tm, stride=SUBLANES), :] for s in range(SUBLANES)], axis=1)


def _store_tiled(ref, val, tm):
    for s in range(SUBLANES):
        ref[pl.ds(s, tm, stride=SUBLANES), :] = val[:, s * LANES:(s + 1) * LANES]


def _token_specs(xs, first_block=0):
    if len(xs) == 1:
        return None, [pl.BlockSpec((TM_TOK * SUBLANES, LANES), lambda i: (first_block + i, 0))]
    n_a = xs[0].shape[0] // TM_TOK
    return n_a, _two_source_specs((TM_TOK, D_MODEL), n_a)


def _load_tokens(n_a, refs):
    if n_a is None:
        return _load_tiled(refs[0], TM_TOK), refs[1:]
    return jnp.where(pl.program_id(0) < n_a, refs[0][...], refs[1][...]), refs[2:]


def _in_proj_kernel(n_a, *refs):
    x, refs = _load_tokens(n_a, refs)
    nm_ref, win_ref, gv_ref, wsp_ref, bsb_ref, ona_ref, ya_ref, qkv_ref, ybuf = refs
    xn = _rms(x, nm_ref[...])
    z = _dot(xn.astype(BF16), win_ref[...])
    qkv_ref[:, 0:D_B] = (z[:, 2 * D_A:2 * D_A + D_B] * (HEAD_DIM ** -0.5)).astype(BF16)
    qkv_ref[:, D_B:3 * D_B] = z[:, 2 * D_A + D_B:].astype(BF16)
    za = jax.nn.gelu(z[:, :2 * D_A])
    u = za[:, :D_A]
    vn = _rms(za[:, D_A:], gv_ref[...]).astype(BF16)
    lane = lax.broadcasted_iota(jnp.int32, (CHUNK, 2 * HEAD_DIM), 1)
    is_lo = lane < HEAD_DIM
    zero = jnp.zeros((CHUNK, 2 * HEAD_DIM), BF16)
    tm = x.shape[0]
    for c2 in range(tm // (2 * CHUNK)):
        for j in range(H_A // 2):
            cols = []
            for cc in range(2):
                r0 = (2 * c2 + cc) * CHUNK
                vp = vn[r0:r0 + CHUNK, j * 128:(j + 1) * 128]
                cols.append(jnp.concatenate([jnp.where(is_lo, vp, zero), jnp.where(is_lo, zero, vp)], axis=0))
            rhs = jnp.concatenate(cols, axis=1)
            sv = _dot(wsp_ref[j], rhs)
            for cc in range(2):
                r0 = (2 * c2 + cc) * CHUNK
                ybuf[r0:r0 + CHUNK, j * 128:(j + 1) * 128] = u[r0:r0 + CHUNK, j * 128:(j + 1) * 128] * (
                    sv[:, cc * 128:(cc + 1) * 128] + bsb_ref[:, j * 128:(j + 1) * 128])
    ya_ref[...] = _rms(ybuf[...], ona_ref[...]).astype(BF16)


def _in_proj(xs, n, nm, win, gv, wsp, bsb, ona):
    const = lambda i: (0, 0)
    n_a, x_specs = _token_specs(xs)
    return pl.pallas_call(
        functools.partial(_in_proj_kernel, n_a),
        grid=(n // TM_TOK,),
        in_specs=x_specs + [
            pl.BlockSpec((1, D_MODEL), const),
            pl.BlockSpec((D_MODEL, D_IN), const),
            pl.BlockSpec((1, D_A), const),
            pl.BlockSpec((H_A // 2, CHUNK, 2 * CHUNK), lambda i: (0, 0, 0)),
            pl.BlockSpec((CHUNK, D_A), const),
            pl.BlockSpec((1, D_A), const),
        ],
        out_specs=[
            pl.BlockSpec((TM_TOK, D_A), lambda i: (i, 0)),
            pl.BlockSpec((TM_TOK, 3 * D_B), lambda i: (i, 0)),
        ],
        out_shape=[jax.ShapeDtypeStruct((n, D_A), BF16), jax.ShapeDtypeStruct((n, 3 * D_B), BF16)],
        scratch_shapes=[pltpu.VMEM((TM_TOK, D_A), F32)],
        compiler_params=pltpu.CompilerParams(dimension_semantics=("arbitrary",), vmem_limit_bytes=VMEM_LIMIT),
        name="in_proj",
    )(*xs, nm, win, gv, wsp, bsb, ona)


def _natten_kernel(rows, q_ref, kp_ref, kc_ref, kn_ref, vp_ref, vc_ref, vn_ref, bias_ref, onb_ref, o_ref,
                   kwin, vwin, obuf, sbuf0, sbuf1, pbuf0, pbuf1, lbuf0, lbuf1):
    i = pl.program_id(1)
    for b, (k_ref, v_ref) in enumerate(((kp_ref, vp_ref), (kc_ref, vc_ref), (kn_ref, vn_ref))):
        kwin[b * TQ:(b + 1) * TQ, :] = k_ref[...]
        vwin[b * TQ:(b + 1) * TQ, :] = v_ref[...]
    lane = lax.broadcasted_iota(jnp.int32, (GRID_W, 128), 1)
    is_lo = lane < HEAD_DIM
    zero = jnp.zeros((GRID_W, 128), BF16)
    nkey = WIN_H * GRID_W
    n_pair = H_B // 2
    sbuf, pbuf, lbuf = (sbuf0, sbuf1), (pbuf0, pbuf1), (lbuf0, lbuf1)

    def window(rr):
        r = i * ROWS_BLK + rr
        rs = jnp.clip(r - WIN_H // 2, 0, rows - WIN_H)
        return r - rs, pl.multiple_of((rs - (i - 1) * ROWS_BLK) * GRID_W, GRID_W)

    def scores(rr, j):
        d, start = window(rr)
        cs = slice(j * 128, (j + 1) * 128)
        q2 = q_ref[rr * GRID_W:(rr + 1) * GRID_W, cs]
        qs = jnp.concatenate([jnp.where(is_lo, q2, zero), jnp.where(is_lo, zero, q2)], axis=0)
        sbuf[rr % 2][j] = _dot_nt(qs, kwin[pl.ds(start, nkey), cs]) + bias_ref[d, j]

    def softmax(rr, j):
        s = sbuf[rr % 2][j]
        p = jnp.exp(s - jnp.max(s, axis=-1, keepdims=True))
        pbuf[rr % 2][j] = p.astype(BF16)
        lbuf[rr % 2][j] = jnp.broadcast_to(1.0 / jnp.sum(p, axis=-1, keepdims=True), (2 * GRID_W, 128))

    def values(rr, j):
        _, start = window(rr)
        cs = slice(j * 128, (j + 1) * 128)
        o = _dot(pbuf[rr % 2][j], vwin[pl.ds(start, nkey), cs]) * lbuf[rr % 2][j]
        obuf[rr * GRID_W:(rr + 1) * GRID_W, cs] = jnp.where(is_lo, o[:GRID_W], o[GRID_W:])

    for t in range(ROWS_BLK + 2):
        for j in range(n_pair):
            if t < ROWS_BLK:
                scores(t, j)
            if 1 <= t <= ROWS_BLK:
                softmax(t - 1, j)
            if t >= 2:
                values(t - 2, j)
    o_ref[...] = _rms(obuf[...], onb_ref[...]).astype(BF16)


def _natten(qkv, bias, onb, n_seq, seq_len):
    rows = seq_len // GRID_W
    nblk = rows // ROWS_BLK
    blk = lambda col, f: pl.BlockSpec((TQ, D_B), lambda b, i: (b * nblk + f(i), col))
    prev = lambda i: jnp.maximum(i - 1, 0)
    cur = lambda i: i
    nxt = lambda i: jnp.minimum(i + 1, nblk - 1)
    return pl.pallas_call(
        functools.partial(_natten_kernel, rows),
        grid=(n_seq, nblk),
        in_specs=[
            blk(0, cur),
            blk(1, prev), blk(1, cur), blk(1, nxt),
            blk(2, prev), blk(2, cur), blk(2, nxt),
            pl.BlockSpec(memory_space=pltpu.VMEM),
            pl.BlockSpec((1, D_B), lambda b, i: (0, 0)),
        ],
        out_specs=pl.BlockSpec((TQ, D_B), lambda b, i: (b * nblk + i, 0)),
        out_shape=jax.ShapeDtypeStruct((n_seq * seq_len, D_B), BF16),
        scratch_shapes=[
            pltpu.VMEM((3 * TQ, D_B), BF16),
            pltpu.VMEM((3 * TQ, D_B), BF16),
            pltpu.VMEM((TQ, D_B), F32),
            pltpu.VMEM((H_B // 2, 2 * GRID_W, WIN_H * GRID_W), F32),
            pltpu.VMEM((H_B // 2, 2 * GRID_W, WIN_H * GRID_W), F32),
            pltpu.VMEM((H_B // 2, 2 * GRID_W, WIN_H * GRID_W), BF16),
            pltpu.VMEM((H_B // 2, 2 * GRID_W, WIN_H * GRID_W), BF16),
            pltpu.VMEM((H_B // 2, 2 * GRID_W, 128), F32),
            pltpu.VMEM((H_B // 2, 2 * GRID_W, 128), F32),
        ],
        compiler_params=pltpu.CompilerParams(dimension_semantics=("arbitrary", "arbitrary"),
                                             vmem_limit_bytes=VMEM_LIMIT),
        name="natten",
    )(qkv, qkv, qkv, qkv, qkv, qkv, qkv, bias, onb)


def _natten_bias(rpb):
    c = np.arange(GRID_W)
    cs = np.clip(c - WIN_W // 2, 0, GRID_W - WIN_W)
    col_mask = (c[None, :] >= cs[:, None]) & (c[None, :] < cs[:, None] + WIN_W)
    col_rel = np.clip(c[None, :] - c[:, None], -(WIN_W - 1), WIN_W - 1) + (WIN_W - 1)
    d = np.arange(WIN_H)
    row_rel = np.arange(WIN_H)[None, :] - d[:, None] + (WIN_H - 1)
    row_sel = np.eye(2 * WIN_H - 1, dtype=np.float32)[row_rel]
    col_sel = np.eye(2 * WIN_W - 1, dtype=np.float32)[col_rel]
    b = jnp.einsum("dia,ckb,hab->dhcik", row_sel, col_sel, rpb.astype(F32), precision=lax.Precision.HIGHEST)
    b = jnp.where(col_mask[None, None, :, None, :], b, NEG)
    return b.reshape(WIN_H, H_B // 2, 2 * GRID_W, WIN_H * GRID_W)


def _out_proj_kernel(n_a, ya_ref, yb_ref, *refs):
    xr, refs = _load_tokens(n_a, refs)
    woa_ref, wob_ref, nf_ref, rwt_ref, rb_ref, x2_ref, ids_ref = refs
    y = _dot(ya_ref[...], woa_ref[...]) + _dot(yb_ref[...], wob_ref[...])
    x2 = xr + y
    _store_tiled(x2_ref, x2, TM_TOK)
    xn = _rms(x2, nf_ref[...])
    lt = _dot_nt(rwt_ref[...], xn.astype(BF16)) + rb_ref[...]
    pj = [lt[8 * j:8 * (j + 1), :] for j in range(E_PER_GROUP)]
    m8 = jnp.maximum(jnp.maximum(pj[0], pj[1]), jnp.maximum(pj[2], pj[3]))
    m = jnp.max(m8, axis=0, keepdims=True)
    e = [jnp.exp(p - m) for p in pj]
    gs = None
    for a, b in PAIRS:
        s = e[a] + e[b]
        gs = s if gs is None else jnp.maximum(gs, s)
    best = gs[0:1, :]
    gsel = jnp.zeros(best.shape, jnp.int32)
    for g in range(1, N_GROUPS):
        cand = gs[g:g + 1, :]
        upd = cand > best
        gsel = jnp.where(upd, g, gsel)
        best = jnp.where(upd, cand, best)
    es = []
    for j in range(E_PER_GROUP):
        v = e[j][0:1, :]
        for g in range(1, N_GROUPS):
            v = jnp.where(gsel == g, e[j][g:g + 1, :], v)
        es.append(v)
    v1 = es[0]
    i1 = jnp.zeros(v1.shape, jnp.int32)
    for j in range(1, E_PER_GROUP):
        upd = es[j] > v1
        i1 = jnp.where(upd, j, i1)
        v1 = jnp.where(upd, es[j], v1)
    v2 = jnp.full(v1.shape, -1.0, F32)
    i2 = jnp.zeros(v1.shape, jnp.int32)
    for j in range(E_PER_GROUP):
        upd = (i1 != j) & (es[j] > v2)
        i2 = jnp.where(upd, j, i2)
        v2 = jnp.where(upd, es[j], v2)
    a = jnp.minimum(i1, i2)
    b = jnp.maximum(i1, i2)
    base = jnp.where(a == 0, 0, jnp.where(a == 1, 3, 5))
    cls = gsel * N_PAIRS + base + (b - a - 1)
    ids_ref[...] = jnp.concatenate([cls, jnp.zeros((SUBLANES - 1,) + cls.shape[1:], jnp.int32)], axis=0)


def _out_proj(ya, yb, xrs, woa, wob, nf, rwt, rb):
    n = ya.shape[0]
    const = lambda i: (0, 0)
    n_a, x_specs = _token_specs(xrs)
    return pl.pallas_call(
        functools.partial(_out_proj_kernel, n_a),
        grid=(n // TM_TOK,),
        in_specs=[
            pl.BlockSpec((TM_TOK, D_A), lambda i: (i, 0)),
            pl.BlockSpec((TM_TOK, D_B), lambda i: (i, 0)),
        ] + x_specs + [
            pl.BlockSpec((D_A, D_MODEL), const),
            pl.BlockSpec((D_B, D_MODEL), const),
            pl.BlockSpec((1, D_MODEL), const),
            pl.BlockSpec((8 * E_PER_GROUP, D_MODEL), const),
            pl.BlockSpec((8 * E_PER_GROUP, 1), const),
        ],
        out_specs=[
            pl.BlockSpec((TM_TOK * SUBLANES, LANES), lambda i: (i, 0)),
            pl.BlockSpec((SUBLANES, TM_TOK), lambda i: (0, i)),
        ],
        out_shape=[
            jax.ShapeDtypeStruct((n * SUBLANES, LANES), F32),
            jax.ShapeDtypeStruct((SUBLANES, n), jnp.int32),
        ],
        compiler_params=pltpu.CompilerParams(dimension_semantics=("arbitrary",), vmem_limit_bytes=VMEM_LIMIT),
        name="out_proj",
    )(ya, yb, *xrs, woa, wob, nf, rwt, rb)


def _moe_kernel(n_tok, ea_ref, eb_ref, row0_ref, cnt_ref, order_ref, nf_ref, rwa_ref, rwb_ref, rba_ref, rbb_ref,
                wga_ref, wua_ref, wda_ref, wgb_ref, wub_ref, wdb_ref, x2_hbm, x3_hbm,
                xbuf, obuf, gsem, ssem):
    j = pl.program_id(0)
    n = pl.num_programs(0)
    slot = j % 2
    unroll = 8

    def tile_of(tok):
        return pl.ds(pl.multiple_of(tok * SUBLANES, SUBLANES), SUBLANES)

    def gather_rows(t, s):
        base = row0_ref[t]

        def body(i, c):
            for k in range(unroll):
                r = i * unroll + k
                pltpu.make_async_copy(x2_hbm.at[tile_of(order_ref[base + r])], xbuf.at[s, tile_of(r)],
                                      gsem.at[s]).start()
            return c
        lax.fori_loop(0, TM_MOE // unroll, body, 0)

    def wait_gather(s):
        pltpu.make_async_copy(x2_hbm.at[pl.ds(0, TM_MOE * SUBLANES)], xbuf.at[s], gsem.at[s]).wait()

    def scatter_rows(t, s):
        base = row0_ref[t]
        cnt = cnt_ref[t]
        pad0 = n_tok + t * TM_MOE - base - cnt

        def rows(dst_of):
            def body(i, c):
                for k in range(unroll):
                    r = i * unroll + k
                    pltpu.make_async_copy(obuf.at[s, tile_of(r)], x3_hbm.at[tile_of(dst_of(r))],
                                          ssem.at[s]).start()
                return c
            lax.fori_loop(0, TM_MOE // unroll, body, 0)

        @pl.when(cnt == TM_MOE)
        def _():
            rows(lambda r: order_ref[base + r])

        @pl.when(cnt < TM_MOE)
        def _():
            rows(lambda r: jnp.where(r < cnt, order_ref[base + r], pad0 + r))

    def wait_scatter(s):
        pltpu.make_async_copy(obuf.at[s], x3_hbm.at[pl.ds(0, TM_MOE * SUBLANES)], ssem.at[s]).wait()

    @pl.when(j == 0)
    def _():
        gather_rows(0, 0)

    wait_gather(slot)

    @pl.when(j + 1 < n)
    def _():
        gather_rows(j + 1, 1 - slot)

    @pl.when(j >= 2)
    def _():
        wait_scatter(slot)

    x = _load_tiled(xbuf.at[slot], TM_MOE)
    xn = _rms(x, nf_ref[...])
    la = jnp.sum(xn * rwa_ref[...], axis=-1, keepdims=True) + rba_ref[:, 0:1]
    lb = jnp.sum(xn * rwb_ref[...], axis=-1, keepdims=True) + rbb_ref[:, 0:1]
    lm = jnp.maximum(la, lb)
    pa = jnp.exp(la - lm)
    pb = jnp.exp(lb - lm)
    tot = pa + pb
    xb = xn.astype(BF16)
    ha = jax.nn.silu(_dot(xb, wga_ref[...])) * _dot(xb, wua_ref[...])
    oa = _dot(ha.astype(BF16), wda_ref[...])
    hb = jax.nn.silu(_dot(xb, wgb_ref[...])) * _dot(xb, wub_ref[...])
    ob = _dot(hb.astype(BF16), wdb_ref[...])
    _store_tiled(obuf.at[slot], x + (pa / tot) * oa + (pb / tot) * ob, TM_MOE)
    scatter_rows(j, slot)

    @pl.when(j == n - 1)
    def _():
        wait_scatter(slot)

        @pl.when(n >= 2)
        def _():
            wait_scatter(1 - slot)


def _moe(x2, plan, nf, rw_rows, rb_rows, wg, wu, wd, n):
    tile_ea, tile_eb, row0, cnt, order = plan
    n_tiles = tile_ea.shape[0]
    n_rows_out = n_tiles * TM_MOE
    wspec = lambda shape, which: pl.BlockSpec(
        (None,) + shape, lambda j, ea, eb, r0, ct, od: ((ea, eb)[which][j], 0, 0))
    grid_spec = pltpu.PrefetchScalarGridSpec(
        num_scalar_prefetch=5,
        grid=(n_tiles,),
        in_specs=[
            pl.BlockSpec((1, D_MODEL), lambda j, *_: (0, 0)),
            wspec((1, D_MODEL), 0), wspec((1, D_MODEL), 1), wspec((1, LANES), 0), wspec((1, LANES), 1),
            wspec((D_MODEL, D_EXPERT), 0), wspec((D_MODEL, D_EXPERT), 0), wspec((D_EXPERT, D_MODEL), 0),
            wspec((D_MODEL, D_EXPERT), 1), wspec((D_MODEL, D_EXPERT), 1), wspec((D_EXPERT, D_MODEL), 1),
            pl.BlockSpec(memory_space=pl.ANY),
        ],
        out_specs=pl.BlockSpec(memory_space=pl.ANY),
        scratch_shapes=[
            pltpu.VMEM((2, TM_MOE * SUBLANES, LANES), F32),
            pltpu.VMEM((2, TM_MOE * SUBLANES, LANES), F32),
            pltpu.SemaphoreType.DMA((2,)),
            pltpu.SemaphoreType.DMA((2,)),
        ],
    )
    return pl.pallas_call(
        functools.partial(_moe_kernel, n),
        grid_spec=grid_spec,
        out_shape=jax.ShapeDtypeStruct((n_rows_out * SUBLANES, LANES), F32),
        compiler_params=pltpu.CompilerParams(dimension_semantics=("arbitrary",), vmem_limit_bytes=VMEM_LIMIT),
        name="moe",
    )(tile_ea, tile_eb, row0, cnt, order, nf, rw_rows, rw_rows, rb_rows, rb_rows, wg, wu, wd, wg, wu, wd, x2)


def _moe_plan(cls, n):
    n_tiles = n // TM_MOE + N_CLASSES
    _, order = lax.sort((cls, lax.iota(jnp.int32, n)), num_keys=1, is_stable=True)
    counts = jnp.sum((cls[None, :] == jnp.arange(N_CLASSES, dtype=jnp.int32)[:, None]).astype(jnp.int32), axis=1)
    tiles_per = (counts + TM_MOE - 1) // TM_MOE
    tile_end = jnp.cumsum(tiles_per)
    sorted_start = jnp.cumsum(counts) - counts
    t = jnp.arange(n_tiles, dtype=jnp.int32)
    tile_cls = jnp.minimum(jnp.sum((tile_end[None, :] <= t[:, None]).astype(jnp.int32), axis=1), N_CLASSES - 1)
    k = t - (tile_end - tiles_per)[tile_cls]
    cnt = jnp.clip(counts[tile_cls] - k * TM_MOE, 0, TM_MOE)
    row0 = jnp.minimum(sorted_start[tile_cls] + k * TM_MOE, n)
    row0 = jnp.where(cnt > 0, row0, jnp.minimum(sorted_start[tile_cls] + counts[tile_cls], n))
    pair_a = jnp.array([p[0] for p in PAIRS], jnp.int32)
    pair_b = jnp.array([p[1] for p in PAIRS], jnp.int32)
    tile_ea = (tile_cls // N_PAIRS) * E_PER_GROUP + pair_a[tile_cls % N_PAIRS]
    tile_eb = (tile_cls // N_PAIRS) * E_PER_GROUP + pair_b[tile_cls % N_PAIRS]
    i32 = lambda v: v.astype(jnp.int32)
    order = jnp.concatenate([order, jnp.zeros((TM_MOE,), jnp.int32)])
    return i32(tile_ea), i32(tile_eb), i32(row0), i32(cnt), i32(order)


def _final_norm_kernel(x_ref, g_ref, o_ref):
    o_ref[...] = _rms(_load_tiled(x_ref, TM_TOK), g_ref[...])


def _final_norm(x, g, row0, n_rows):
    blk0 = row0 // TM_TOK
    return pl.pallas_call(
        _final_norm_kernel,
        grid=(n_rows // TM_TOK,),
        in_specs=[pl.BlockSpec((TM_TOK * SUBLANES, LANES), lambda i: (blk0 + i, 0)),
                  pl.BlockSpec((1, D_MODEL), lambda i: (0, 0))],
        out_specs=pl.BlockSpec((TM_TOK, D_MODEL), lambda i: (i, 0)),
        out_shape=jax.ShapeDtypeStruct((n_rows, D_MODEL), F32),
        compiler_params=pltpu.CompilerParams(dimension_semantics=("arbitrary",), vmem_limit_bytes=VMEM_LIMIT),
        name="final_norm",
    )(x, g)


def _trunk(xs, n_seq, seq_len, norm_mix, w_in, gmlp_v_norm, gmlp_ws, gmlp_bs, na_rpb, out_norm_a, out_norm_b,
           w_out, norm_ffn, router_w, router_bias, w_gate, w_up, w_down):
    n = n_seq * seq_len
    row = lambda v: v.reshape(1, -1).astype(F32)
    perm = np.array([[E_PER_GROUP * g + j for g in range(N_GROUPS)] for j in range(E_PER_GROUP)])
    rwt = jnp.zeros((E_PER_GROUP, 8, D_MODEL), F32).at[:, :N_GROUPS, :].set(jnp.transpose(router_w)[perm])
    rwt = rwt.reshape(8 * E_PER_GROUP, D_MODEL).astype(BF16)
    rb = jnp.full((E_PER_GROUP, 8), NEG, F32).at[:, :N_GROUPS].set(router_bias.astype(F32)[perm])
    rb = rb.reshape(8 * E_PER_GROUP, 1)
    rw_rows = jnp.transpose(router_w).astype(F32).reshape(N_EXPERTS, 1, D_MODEL)
    rb_rows = jnp.broadcast_to(router_bias.astype(F32).reshape(N_EXPERTS, 1, 1), (N_EXPERTS, 1, LANES))
    for l in range(DEPTH):
        ws = gmlp_ws[l].astype(BF16)
        wsp = jnp.concatenate([ws[0::2], ws[1::2]], axis=2)
        bsb = jnp.repeat(jnp.transpose(gmlp_bs[l]).astype(F32), HEAD_DIM, axis=1)
        ya, qkv = _in_proj(xs, n, row(norm_mix[l]), w_in[l].astype(BF16), row(gmlp_v_norm[l]), wsp, bsb,
                           row(out_norm_a[l]))
        yb = _natten(qkv, _natten_bias(na_rpb[l]), row(out_norm_b[l]), n_seq, seq_len)
        wo = w_out[l].astype(BF16)
        x2, ids = _out_proj(ya, yb, xs, wo[:D_A], wo[D_A:], row(norm_ffn[l]), rwt, rb)
        xs = [_moe(x2, _moe_plan(ids[0], n), row(norm_ffn[l]), rw_rows, rb_rows, w_gate[l].astype(BF16),
                   w_up[l].astype(BF16), w_down[l].astype(BF16), n)]
    return xs[0]


def kernel(x_prompt, x_sample, norm_mix, w_in, gmlp_v_norm, gmlp_ws, gmlp_bs, na_rpb, out_norm_a, out_norm_b,
           w_out, norm_ffn, router_w, router_bias, w_gate, w_up, w_down, norm_final):
    bp, sp, _ = x_prompt.shape
    bs, ss, _ = x_sample.shape
    assert sp == ss and (bp * sp) % TM_TOK == 0
    xs = [x_prompt.reshape(bp * sp, D_MODEL), x_sample.reshape(bs * ss, D_MODEL)]
    x = _trunk(xs, bp + bs, sp, norm_mix, w_in, gmlp_v_norm, gmlp_ws, gmlp_bs, na_rpb, out_norm_a, out_norm_b,
               w_out, norm_ffn, router_w, router_bias, w_gate, w_up, w_down)
    g = norm_final.reshape(1, -1).astype(F32)
    y_prompt = _final_norm(x, g, 0, bp * sp).reshape(bp, sp, D_MODEL)
    y_sample = _final_norm(x, g, bp * sp, bs * ss).reshape(bs, ss, D_MODEL)
    return (y_prompt, y_sample)
```

```python
import functools

import jax
import jax.numpy as jnp
import numpy as np
from jax import lax
from jax.experimental import pallas as pl
from jax.experimental.pallas import tpu as pltpu

F32 = jnp.float32
BF16 = jnp.bfloat16

D_MODEL = 1024
DEPTH = 2
D_A = 512
D_B = 512
HEAD_DIM = 64
H_A = D_A // HEAD_DIM
H_B = D_B // HEAD_DIM
CHUNK = 128
GRID_W = 64
WIN_H = 8
WIN_W = 16
N_EXPERTS = 16
N_GROUPS = 4
E_PER_GROUP = 4
D_EXPERT = 512
D_IN = 2 * D_A + 3 * D_B
EPS = 1e-6
NEG = -1e30

PAIRS = ((0, 1), (0, 2), (0, 3), (1, 2), (1, 3), (2, 3))
N_PAIRS = len(PAIRS)
N_CLASSES = N_GROUPS * N_PAIRS

SUBLANES = 8
LANES = 128
TM_TOK = 512
ROWS_BLK = 8
TQ = ROWS_BLK * GRID_W
TM_MOE = 512
VMEM_LIMIT = 56 * 1024 * 1024


def _rms(x, g):
    return x * lax.rsqrt(jnp.mean(x * x, axis=-1, keepdims=True) + EPS) * g


def _dot(a, b):
    return jnp.dot(a, b, preferred_element_type=F32)


def _dot_nt(a, b):
    return lax.dot_general(a, b, (((1,), (1,)), ((), ())), preferred_element_type=F32)


def _two_source_specs(shape, n_a):
    return [pl.BlockSpec(shape, lambda i: (jnp.minimum(i, n_a - 1), 0)),
            pl.BlockSpec(shape, lambda i: (jnp.maximum(i - n_a, 0), 0))]


def _load_tiled(ref, tm):
    return jnp.concatenate([ref[pl.ds(s, tm, stride=SUBLANES), :] for s in range(SUBLANES)], axis=1)


def _store_tiled(ref, val, tm):
    for s in range(SUBLANES):
        ref[pl.ds(s, tm, stride=SUBLANES), :] = val[:, s * LANES:(s + 1) * LANES]


def _token_specs(xs, first_block=0):
    if len(xs) == 1:
        return None, [pl.BlockSpec((TM_TOK * SUBLANES, LANES), lambda i: (first_block + i, 0))]
    n_a = xs[0].shape[0] // TM_TOK
    return n_a, _two_source_specs((TM_TOK, D_MODEL), n_a)


def _load_tokens(n_a, refs):
    if n_a is None:
        return _load_tiled(refs[0], TM_TOK), refs[1:]
    return jnp.where(pl.program_id(0) < n_a, refs[0][...], refs[1][...]), refs[2:]


def _in_proj_kernel(n_a, *refs):
    x, refs = _load_tokens(n_a, refs)
    nm_ref, win_ref, gv_ref, wsp_ref, bsb_ref, ona_ref, ya_ref, qkv_ref, ybuf = refs
    xn = _rms(x, nm_ref[...])
    z = _dot(xn.astype(BF16), win_ref[...])
    qkv_ref[:, 0:D_B] = (z[:, 2 * D_A:2 * D_A + D_B] * (HEAD_DIM ** -0.5)).astype(BF16)
    qkv_ref[:, D_B:3 * D_B] = z[:, 2 * D_A + D_B:].astype(BF16)
    za = jax.nn.gelu(z[:, :2 * D_A])
    u = za[:, :D_A]
    vn = _rms(za[:, D_A:], gv_ref[...]).astype(BF16)
    lane = lax.broadcasted_iota(jnp.int32, (CHUNK, 2 * HEAD_DIM), 1)
    is_lo = lane < HEAD_DIM
    zero = jnp.zeros((CHUNK, 2 * HEAD_DIM), BF16)
    tm = x.shape[0]
    for c2 in range(tm // (2 * CHUNK)):
        for j in range(H_A // 2):
            cols = []
            for cc in range(2):
                r0 = (2 * c2 + cc) * CHUNK
                vp = vn[r0:r0 + CHUNK, j * 128:(j + 1) * 128]
                cols.append(jnp.concatenate([jnp.where(is_lo, vp, zero), jnp.where(is_lo, zero, vp)], axis=0))
            rhs = jnp.concatenate(cols, axis=1)
            sv = _dot(wsp_ref[j], rhs)
            for cc in range(2):
                r0 = (2 * c2 + cc) * CHUNK
                ybuf[r0:r0 + CHUNK, j * 128:(j + 1) * 128] = u[r0:r0 + CHUNK, j * 128:(j + 1) * 128] * (
                    sv[:, cc * 128:(cc + 1) * 128] + bsb_ref[:, j * 128:(j + 1) * 128])
    ya_ref[...] = _rms(ybuf[...], ona_ref[...]).astype(BF16)


def _in_proj(xs, n, nm, win, gv, wsp, bsb, ona):
    const = lambda i: (0, 0)
    n_a, x_specs = _token_specs(xs)
    return pl.pallas_call(
        functools.partial(_in_proj_kernel, n_a),
        grid=(n // TM_TOK,),
        in_specs=x_specs + [
            pl.BlockSpec((1, D_MODEL), const),
            pl.BlockSpec((D_MODEL, D_IN), const),
            pl.BlockSpec((1, D_A), const),
            pl.BlockSpec((H_A // 2, CHUNK, 2 * CHUNK), lambda i: (0, 0, 0)),
            pl.BlockSpec((CHUNK, D_A), const),
            pl.BlockSpec((1, D_A), const),
        ],
        out_specs=[
            pl.BlockSpec((TM_TOK, D_A), lambda i: (i, 0)),
            pl.BlockSpec((TM_TOK, 3 * D_B), lambda i: (i, 0)),
        ],
        out_shape=[jax.ShapeDtypeStruct((n, D_A), BF16), jax.ShapeDtypeStruct((n, 3 * D_B), BF16)],
        scratch_shapes=[pltpu.VMEM((TM_TOK, D_A), F32)],
        compiler_params=pltpu.CompilerParams(dimension_semantics=("arbitrary",), vmem_limit_bytes=VMEM_LIMIT),
        name="in_proj",
    )(*xs, nm, win, gv, wsp, bsb, ona)


def _natten_kernel(rows, q_ref, kp_ref, kc_ref, kn_ref, vp_ref, vc_ref, vn_ref, bias_ref, onb_ref, o_ref,
                   kwin, vwin, obuf, sbuf0, sbuf1, pbuf0, pbuf1, lbuf0, lbuf1):
    i = pl.program_id(1)
    for b, (k_ref, v_ref) in enumerate(((kp_ref, vp_ref), (kc_ref, vc_ref), (kn_ref, vn_ref))):
        kwin[b * TQ:(b + 1) * TQ, :] = k_ref[...]
        vwin[b * TQ:(b + 1) * TQ, :] = v_ref[...]
    lane = lax.broadcasted_iota(jnp.int32, (GRID_W, 128), 1)
    is_lo = lane < HEAD_DIM
    zero = jnp.zeros((GRID_W, 128), BF16)
    nkey = WIN_H * GRID_W
    n_pair = H_B // 2
    sbuf, pbuf, lbuf = (sbuf0, sbuf1), (pbuf0, pbuf1), (lbuf0, lbuf1)

    def window(rr):
        r = i * ROWS_BLK + rr
        rs = jnp.clip(r - WIN_H // 2, 0, rows - WIN_H)
        return r - rs, pl.multiple_of((rs - (i - 1) * ROWS_BLK) * GRID_W, GRID_W)

    def scores(rr, j):
        d, start = window(rr)
        cs = slice(j * 128, (j + 1) * 128)
        q2 = q_ref[rr * GRID_W:(rr + 1) * GRID_W, cs]
        qs = jnp.concatenate([jnp.where(is_lo, q2, zero), jnp.where(is_lo, zero, q2)], axis=0)
        sbuf[rr % 2][j] = _dot_nt(qs, kwin[pl.ds(start, nkey), cs]) + bias_ref[d, j]

    def softmax(rr, j):
        s = sbuf[rr % 2][j]
        p = jnp.exp(s - jnp.max(s, axis=-1, keepdims=True))
        pbuf[rr % 2][j] = p.astype(BF16)
        lbuf[rr % 2][j] = jnp.broadcast_to(1.0 / jnp.sum(p, axis=-1, keepdims=True), (2 * GRID_W, 128))

    def values(rr, j):
        _, start = window(rr)
        cs = slice(j * 128, (j + 1) * 128)
        o = _dot(pbuf[rr % 2][j], vwin[pl.ds(start, nkey), cs]) * lbuf[rr % 2][j]
        obuf[rr * GRID_W:(rr + 1) * GRID_W, cs] = jnp.where(is_lo, o[:GRID_W], o[GRID_W:])

    for t in range(ROWS_BLK + 2):
        for j in range(n_pair):
            if t < ROWS_BLK:
                scores(t, j)
            if 1 <= t <= ROWS_BLK:
                softmax(t - 1, j)
            if t >= 2:
                values(t - 2, j)
    o_ref[...] = _rms(obuf[...], onb_ref[...]).astype(BF16)


def _natten(qkv, bias, onb, n_seq, seq_len):
    rows = seq_len // GRID_W
    nblk = rows // ROWS_BLK
    blk = lambda col, f: pl.BlockSpec((TQ, D_B), lambda b, i: (b * nblk + f(i), col))
    prev = lambda i: jnp.maximum(i - 1, 0)
    cur = lambda i: i
    nxt = lambda i: jnp.minimum(i + 1, nblk - 1)
    return pl.pallas_call(
        functools.partial(_natten_kernel, rows),
        grid=(n_seq, nblk),
        in_specs=[
            blk(0, cur),
            blk(1, prev), blk(1, cur), blk(1, nxt),
            blk(2, prev), blk(2, cur), blk(2, nxt),
            pl.BlockSpec(memory_space=pltpu.VMEM),
            pl.BlockSpec((1, D_B), lambda b, i: (0, 0)),
        ],
        out_specs=pl.BlockSpec((TQ, D_B), lambda b, i: (b * nblk + i, 0)),
        out_shape=jax.ShapeDtypeStruct((n_seq * seq_len, D_B), BF16),
        scratch_shapes=[
            pltpu.VMEM((3 * TQ, D_B), BF16),
            pltpu.VMEM((3 * TQ, D_B), BF16),
            pltpu.VMEM((TQ, D_B), F32),
            pltpu.VMEM((H_B // 2, 2 * GRID_W, WIN_H * GRID_W), F32),
            pltpu.VMEM((H_B // 2, 2 * GRID_W, WIN_H * GRID_W), F32),
            pltpu.VMEM((H_B // 2, 2 * GRID_W, WIN_H * GRID_W), BF16),
            pltpu.VMEM((H_B // 2, 2 * GRID_W, WIN_H * GRID_W), BF16),
            pltpu.VMEM((H_B // 2, 2 * GRID_W, 128), F32),
            pltpu.VMEM((H_B // 2, 2 * GRID_W, 128), F32),
        ],
        compiler_params=pltpu.CompilerParams(dimension_semantics=("arbitrary", "arbitrary"),
                                             vmem_limit_bytes=VMEM_LIMIT),
        name="natten",
    )(qkv, qkv, qkv, qkv, qkv, qkv, qkv, bias, onb)


def _natten_bias(rpb):
    c = np.arange(GRID_W)
    cs = np.clip(c - WIN_W // 2, 0, GRID_W - WIN_W)
    col_mask = (c[None, :] >= cs[:, None]) & (c[None, :] < cs[:, None] + WIN_W)
    col_rel = np.clip(c[None, :] - c[:, None], -(WIN_W - 1), WIN_W - 1) + (WIN_W - 1)
    d = np.arange(WIN_H)
    row_rel = np.arange(WIN_H)[None, :] - d[:, None] + (WIN_H - 1)
    row_sel = np.eye(2 * WIN_H - 1, dtype=np.float32)[row_rel]
    col_sel = np.eye(2 * WIN_W - 1, dtype=np.float32)[col_rel]
    b = jnp.einsum("dia,ckb,hab->dhcik", row_sel, col_sel, rpb.astype(F32), precision=lax.Precision.HIGHEST)
    b = jnp.where(col_mask[None, None, :, None, :], b, NEG)
    return b.reshape(WIN_H, H_B // 2, 2 * GRID_W, WIN_H * GRID_W)


def _out_proj_kernel(n_a, ya_ref, yb_ref, *refs):
    xr, refs = _load_tokens(n_a, refs)
    woa_ref, wob_ref, nf_ref, rwt_ref, rb_ref, x2_ref, ids_ref = refs
    y = _dot(ya_ref[...], woa_ref[...]) + _dot(yb_ref[...], wob_ref[...])
    x2 = xr + y
    _store_tiled(x2_ref, x2, TM_TOK)
    xn = _rms(x2, nf_ref[...])
    lt = _dot_nt(rwt_ref[...], xn.astype(BF16)) + rb_ref[...]
    pj = [lt[8 * j:8 * (j + 1), :] for j in range(E_PER_GROUP)]
    m8 = jnp.maximum(jnp.maximum(pj[0], pj[1]), jnp.maximum(pj[2], pj[3]))
    m = jnp.max(m8, axis=0, keepdims=True)
    e = [jnp.exp(p - m) for p in pj]
    gs = None
    for a, b in PAIRS:
        s = e[a] + e[b]
        gs = s if gs is None else jnp.maximum(gs, s)
    best = gs[0:1, :]
    gsel = jnp.zeros(best.shape, jnp.int32)
    for g in range(1, N_GROUPS):
        cand = gs[g:g + 1, :]
        upd = cand > best
        gsel = jnp.where(upd, g, gsel)
        best = jnp.where(upd, cand, best)
    es = []
    for j in range(E_PER_GROUP):
        v = e[j][0:1, :]
        for g in range(1, N_GROUPS):
            v = jnp.where(gsel == g, e[j][g:g + 1, :], v)
        es.append(v)
    v1 = es[0]
    i1 = jnp.zeros(v1.shape, jnp.int32)
    for j in range(1, E_PER_GROUP):
        upd = es[j] > v1
        i1 = jnp.where(upd, j, i1)
        v1 = jnp.where(upd, es[j], v1)
    v2 = jnp.full(v1.shape, -1.0, F32)
    i2 = jnp.zeros(v1.shape, jnp.int32)
    for j in range(E_PER_GROUP):
        upd = (i1 != j) & (es[j] > v2)
        i2 = jnp.where(upd, j, i2)
        v2 = jnp.where(upd, es[j], v2)
    a = jnp.minimum(i1, i2)
    b = jnp.maximum(i1, i2)
    base = jnp.where(a == 0, 0, jnp.where(a == 1, 3, 5))
    cls = gsel * N_PAIRS + base + (b - a - 1)
    ids_ref[...] = jnp.concatenate([cls, jnp.zeros((SUBLANES - 1,) + cls.shape[1:], jnp.int32)], axis=0)


def _out_proj(ya, yb, xrs, woa, wob, nf, rwt, rb):
    n = ya.shape[0]
    const = lambda i: (0, 0)
    n_a, x_specs = _token_specs(xrs)
    return pl.pallas_call(
        functools.partial(_out_proj_kernel, n_a),
        grid=(n // TM_TOK,),
        in_specs=[
            pl.BlockSpec((TM_TOK, D_A), lambda i: (i, 0)),
            pl.BlockSpec((TM_TOK, D_B), lambda i: (i, 0)),
        ] + x_specs + [
            pl.BlockSpec((D_A, D_MODEL), const),
            pl.BlockSpec((D_B, D_MODEL), const),
            pl.BlockSpec((1, D_MODEL), const),
            pl.BlockSpec((8 * E_PER_GROUP, D_MODEL), const),
            pl.BlockSpec((8 * E_PER_GROUP, 1), const),
        ],
        out_specs=[
            pl.BlockSpec((TM_TOK * SUBLANES, LANES), lambda i: (i, 0)),
            pl.BlockSpec((SUBLANES, TM_TOK), lambda i: (0, i)),
        ],
        out_shape=[
            jax.ShapeDtypeStruct((n * SUBLANES, LANES), F32),
            jax.ShapeDtypeStruct((SUBLANES, n), jnp.int32),
        ],
        compiler_params=pltpu.CompilerParams(dimension_semantics=("arbitrary",), vmem_limit_bytes=VMEM_LIMIT),
        name="out_proj",
    )(ya, yb, *xrs, woa, wob, nf, rwt, rb)


def _moe_kernel(n_tok, ea_ref, eb_ref, row0_ref, cnt_ref, order_ref, nf_ref, rwa_ref, rwb_ref, rba_ref, rbb_ref,
                wga_ref, wua_ref, wda_ref, wgb_ref, wub_ref, wdb_ref, x2_hbm, x3_hbm,
                xbuf, obuf, gsem, ssem):
    j = pl.program_id(0)
    n = pl.num_programs(0)
    slot = j % 2
    unroll = 8

    def tile_of(tok):
        return pl.ds(pl.multiple_of(tok * SUBLANES, SUBLANES), SUBLANES)

    def gather_rows(t, s):
        base = row0_ref[t]

        def body(i, c):
            for k in range(unroll):
                r = i * unroll + k
                pltpu.make_async_copy(x2_hbm.at[tile_of(order_ref[base + r])], xbuf.at[s, tile_of(r)],
                                      gsem.at[s]).start()
            return c
        lax.fori_loop(0, TM_MOE // unroll, body, 0)

    def wait_gather(s):
        pltpu.make_async_copy(x2_hbm.at[pl.ds(0, TM_MOE * SUBLANES)], xbuf.at[s], gsem.at[s]).wait()

    def scatter_rows(t, s):
        base = row0_ref[t]
        cnt = cnt_ref[t]
        pad0 = n_tok + t * TM_MOE - base - cnt

        def rows(dst_of):
            def body(i, c):
                for k in range(unroll):
                    r = i * unroll + k
                    pltpu.make_async_copy(obuf.at[s, tile_of(r)], x3_hbm.at[tile_of(dst_of(r))],
                                          ssem.at[s]).start()
                return c
            lax.fori_loop(0, TM_MOE // unroll, body, 0)

        @pl.when(cnt == TM_MOE)
        def _():
            rows(lambda r: order_ref[base + r])

        @pl.when(cnt < TM_MOE)
        def _():
            rows(lambda r: jnp.where(r < cnt, order_ref[base + r], pad0 + r))

    def wait_scatter(s):
        pltpu.make_async_copy(obuf.at[s], x3_hbm.at[pl.ds(0, TM_MOE * SUBLANES)], ssem.at[s]).wait()

    @pl.when(j == 0)
    def _():
        gather_rows(0, 0)

    wait_gather(slot)

    @pl.when(j + 1 < n)
    def _():
        gather_rows(j + 1, 1 - slot)

    @pl.when(j >= 2)
    def _():
        wait_scatter(slot)

    @pl.when(cnt_ref[j] > 0)
    def _():
        x = _load_tiled(xbuf.at[slot], TM_MOE)
        xn = _rms(x, nf_ref[...])
        la = jnp.sum(xn * rwa_ref[...], axis=-1, keepdims=True) + rba_ref[:, 0:1]
        lb = jnp.sum(xn * rwb_ref[...], axis=-1, keepdims=True) + rbb_ref[:, 0:1]
        lm = jnp.maximum(la, lb)
        pa = jnp.exp(la - lm)
        pb = jnp.exp(lb - lm)
        tot = pa + pb
        xb = xn.astype(BF16)
        ha = jax.nn.silu(_dot(xb, wga_ref[...])) * _dot(xb, wua_ref[...])
        oa = _dot(ha.astype(BF16), wda_ref[...])
        hb = jax.nn.silu(_dot(xb, wgb_ref[...])) * _dot(xb, wub_ref[...])
        ob = _dot(hb.astype(BF16), wdb_ref[...])
        _store_tiled(obuf.at[slot], x + (pa / tot) * oa + (pb / tot) * ob, TM_MOE)

    scatter_rows(j, slot)

    @pl.when(j == n - 1)
    def _():
        wait_scatter(slot)

        @pl.when(n >= 2)
        def _():
            wait_scatter(1 - slot)


def _moe(x2, plan, nf, rw_rows, rb_rows, wg, wu, wd, n):
    tile_ea, tile_eb, row0, cnt, order = plan
    n_tiles = tile_ea.shape[0]
    assert n // TM_MOE >= 2
    n_rows_out = n_tiles * TM_MOE
    wspec = lambda shape, which: pl.BlockSpec(
        (None,) + shape, lambda j, ea, eb, r0, ct, od: ((ea, eb)[which][j], 0, 0))
    grid_spec = pltpu.PrefetchScalarGridSpec(
        num_scalar_prefetch=5,
        grid=(n_tiles,),
        in_specs=[
            pl.BlockSpec((1, D_MODEL), lambda j, *_: (0, 0)),
            wspec((1, D_MODEL), 0), wspec((1, D_MODEL), 1), wspec((1, LANES), 0), wspec((1, LANES), 1),
            wspec((D_MODEL, D_EXPERT), 0), wspec((D_MODEL, D_EXPERT), 0), wspec((D_EXPERT, D_MODEL), 0),
            wspec((D_MODEL, D_EXPERT), 1), wspec((D_MODEL, D_EXPERT), 1), wspec((D_EXPERT, D_MODEL), 1),
            pl.BlockSpec(memory_space=pl.ANY),
        ],
        out_specs=pl.BlockSpec(memory_space=pl.ANY),
        scratch_shapes=[
            pltpu.VMEM((2, TM_MOE * SUBLANES, LANES), F32),
            pltpu.VMEM((2, TM_MOE * SUBLANES, LANES), F32),
            pltpu.SemaphoreType.DMA((2,)),
            pltpu.SemaphoreType.DMA((2,)),
        ],
    )
    return pl.pallas_call(
        functools.partial(_moe_kernel, n),
        grid_spec=grid_spec,
        out_shape=jax.ShapeDtypeStruct((n_rows_out * SUBLANES, LANES), F32),
        compiler_params=pltpu.CompilerParams(dimension_semantics=("arbitrary",), vmem_limit_bytes=VMEM_LIMIT),
        name="moe",
    )(tile_ea, tile_eb, row0, cnt, order, nf, rw_rows, rw_rows, rb_rows, rb_rows, wg, wu, wd, wg, wu, wd, x2)


def _moe_plan(cls, n):
    n_tiles = n // TM_MOE + N_CLASSES
    _, order = lax.sort((cls, lax.iota(jnp.int32, n)), num_keys=1, is_stable=True)
    counts = jnp.sum((cls[None, :] == jnp.arange(N_CLASSES, dtype=jnp.int32)[:, None]).astype(jnp.int32), axis=1)
    tiles_per = (counts + TM_MOE - 1) // TM_MOE
    tile_end = jnp.cumsum(tiles_per)
    sorted_start = jnp.cumsum(counts) - counts
    t = jnp.arange(n_tiles, dtype=jnp.int32)
    tile_cls = jnp.minimum(jnp.sum((tile_end[None, :] <= t[:, None]).astype(jnp.int32), axis=1), N_CLASSES - 1)
    k = t - (tile_end - tiles_per)[tile_cls]
    cnt = jnp.clip(counts[tile_cls] - k * TM_MOE, 0, TM_MOE)
    row0 = jnp.minimum(sorted_start[tile_cls] + k * TM_MOE, n)
    row0 = jnp.where(cnt > 0, row0, jnp.minimum(sorted_start[tile_cls] + counts[tile_cls], n))
    pair_a = jnp.array([p[0] for p in PAIRS], jnp.int32)
    pair_b = jnp.array([p[1] for p in PAIRS], jnp.int32)
    tile_ea = (tile_cls // N_PAIRS) * E_PER_GROUP + pair_a[tile_cls % N_PAIRS]
    tile_eb = (tile_cls // N_PAIRS) * E_PER_GROUP + pair_b[tile_cls % N_PAIRS]
    i32 = lambda v: v.astype(jnp.int32)
    order = jnp.concatenate([order, jnp.zeros((TM_MOE,), jnp.int32)])
    return i32(tile_ea), i32(tile_eb), i32(row0), i32(cnt), i32(order)


def _final_norm_kernel(x_ref, g_ref, o_ref):
    o_ref[...] = _rms(_load_tiled(x_ref, TM_TOK), g_ref[...])


def _final_norm(x, g, row0, n_rows):
    blk0 = row0 // TM_TOK
    return pl.pallas_call(
        _final_norm_kernel,
        grid=(n_rows // TM_TOK,),
        in_specs=[pl.BlockSpec((TM_TOK * SUBLANES, LANES), lambda i: (blk0 + i, 0)),
                  pl.BlockSpec((1, D_MODEL), lambda i: (0, 0))],
        out_specs=pl.BlockSpec((TM_TOK, D_MODEL), lambda i: (i, 0)),
        out_shape=jax.ShapeDtypeStruct((n_rows, D_MODEL), F32),
        compiler_params=pltpu.CompilerParams(dimension_semantics=("arbitrary",), vmem_limit_bytes=VMEM_LIMIT),
        name="final_norm",
    )(x, g)


def _trunk(xs, n_seq, seq_len, norm_mix, w_in, gmlp_v_norm, gmlp_ws, gmlp_bs, na_rpb, out_norm_a, out_norm_b,
           w_out, norm_ffn, router_w, router_bias, w_gate, w_up, w_down):
    n = n_seq * seq_len
    row = lambda v: v.reshape(1, -1).astype(F32)
    perm = np.array([[E_PER_GROUP * g + j for g in range(N_GROUPS)] for j in range(E_PER_GROUP)])
    rwt = jnp.zeros((E_PER_GROUP, 8, D_MODEL), F32).at[:, :N_GROUPS, :].set(jnp.transpose(router_w)[perm])
    rwt = rwt.reshape(8 * E_PER_GROUP, D_MODEL).astype(BF16)
    rb = jnp.full((E_PER_GROUP, 8), NEG, F32).at[:, :N_GROUPS].set(router_bias.astype(F32)[perm])
    rb = rb.reshape(8 * E_PER_GROUP, 1)
    rw_rows = jnp.transpose(router_w).astype(F32).reshape(N_EXPERTS, 1, D_MODEL)
    rb_rows = jnp.broadcast_to(router_bias.astype(F32).reshape(N_EXPERTS, 1, 1), (N_EXPERTS, 1, LANES))
    for l in range(DEPTH):
        ws = gmlp_ws[l].astype(BF16)
        wsp = jnp.concatenate([ws[0::2], ws[1::2]], axis=2)
        bsb = jnp.repeat(jnp.transpose(gmlp_bs[l]).astype(F32), HEAD_DIM, axis=1)
        ya, qkv = _in_proj(xs, n, row(norm_mix[l]), w_in[l].astype(BF16), row(gmlp_v_norm[l]), wsp, bsb,
                           row(out_norm_a[l]))
        yb = _natten(qkv, _natten_bias(na_rpb[l]), row(out_norm_b[l]), n_seq, seq_len)
        wo = w_out[l].astype(BF16)
        x2, ids = _out_proj(ya, yb, xs, wo[:D_A], wo[D_A:], row(norm_ffn[l]), rwt, rb)
        xs = [_moe(x2, _moe_plan(ids[0], n), row(norm_ffn[l]), rw_rows, rb_rows, w_gate[l].astype(BF16),
                   w_up[l].astype(BF16), w_down[l].astype(BF16), n)]
    return xs[0]


def kernel(x_prompt, x_sample, norm_mix, w_in, gmlp_v_norm, gmlp_ws, gmlp_bs, na_rpb, out_norm_a, out_norm_b,
           w_out, norm_ffn, router_w, router_bias, w_gate, w_up, w_down, norm_final):
    bp, sp, _ = x_prompt.shape
    bs, ss, _ = x_sample.shape
    assert sp == ss and (bp * sp) % TM_TOK == 0
    xs = [x_prompt.reshape(bp * sp, D_MODEL), x_sample.reshape(bs * ss, D_MODEL)]
    x = _trunk(xs, bp + bs, sp, norm_mix, w_in, gmlp_v_norm, gmlp_ws, gmlp_bs, na_rpb, out_norm_a, out_norm_b,
               w_out, norm_ffn, router_w, router_bias, w_gate, w_up, w_down)
    g = norm_final.reshape(1, -1).astype(F32)
    y_prompt = _final_norm(x, g, 0, bp * sp).reshape(bp, sp, D_MODEL)
    y_sample = _final_norm(x, g, bp * sp, bs * ss).reshape(bs, ss, D_MODEL)
    return (y_prompt, y_sample)
```

```python
import functools

import jax
import jax.numpy as jnp
import numpy as np
from jax import lax
from jax.experimental import pallas as pl
from jax.experimental.pallas import tpu as pltpu

F32 = jnp.float32
BF16 = jnp.bfloat16

D_MODEL = 1024
DEPTH = 2
D_A = 512
D_B = 512
HEAD_DIM = 64
H_A = D_A // HEAD_DIM
H_B = D_B // HEAD_DIM
CHUNK = 128
GRID_W = 64
WIN_H = 8
WIN_W = 16
N_EXPERTS = 16
N_GROUPS = 4
E_PER_GROUP = 4
D_EXPERT = 512
D_IN = 2 * D_A + 3 * D_B
EPS = 1e-6
NEG = -1e30

PAIRS = ((0, 1), (0, 2), (0, 3), (1, 2), (1, 3), (2, 3))
N_PAIRS = len(PAIRS)
N_CLASSES = N_GROUPS * N_PAIRS

SUBLANES = 8
LANES = 128
TM_TOK = 1024
ROWS_BLK = 8
TQ = ROWS_BLK * GRID_W
TM_MOE = 256
VMEM_LIMIT = 56 * 1024 * 1024


def _rms(x, g):
    return x * lax.rsqrt(jnp.mean(x * x, axis=-1, keepdims=True) + EPS) * g


def _dot(a, b):
    return jnp.dot(a, b, preferred_element_type=F32)


def _dot_nt(a, b):
    return lax.dot_general(a, b, (((1,), (1,)), ((), ())), preferred_element_type=F32)


def _two_source_specs(shape, n_a):
    return [pl.BlockSpec(shape, lambda i: (jnp.minimum(i, n_a - 1), 0)),
            pl.BlockSpec(shape, lambda i: (jnp.maximum(i - n_a, 0), 0))]


def _load_tiled(ref, tm):
    return jnp.concatenate([ref[pl.ds(s, tm, stride=SUBLANES), :] for s in range(SUBLANES)], axis=1)


def _store_tiled(ref, val, tm):
    for s in range(SUBLANES):
        ref[pl.ds(s, tm, stride=SUBLANES), :] = val[:, s * LANES:(s + 1) * LANES]


def _token_specs(xs, first_block=0):
    if len(xs) == 1:
        return None, [pl.BlockSpec((TM_TOK * SUBLANES, LANES), lambda i: (first_block + i, 0))]
    n_a = xs[0].shape[0] // TM_TOK
    return n_a, _two_source_specs((TM_TOK, D_MODEL), n_a)


def _load_tokens(n_a, refs):
    if n_a is None:
        return _load_tiled(refs[0], TM_TOK), refs[1:]
    return jnp.where(pl.program_id(0) < n_a, refs[0][...], refs[1][...]), refs[2:]


def _in_proj_kernel(n_a, *refs):
    x, refs = _load_tokens(n_a, refs)
    nm_ref, win_ref, gv_ref, wsp_ref, bsb_ref, ona_ref, ya_ref, qkv_ref, ybuf = refs
    xn = _rms(x, nm_ref[...])
    z = _dot(xn.astype(BF16), win_ref[...])
    qkv_ref[:, 0:D_B] = (z[:, 2 * D_A:2 * D_A + D_B] * (HEAD_DIM ** -0.5)).astype(BF16)
    qkv_ref[:, D_B:3 * D_B] = z[:, 2 * D_A + D_B:].astype(BF16)
    za = jax.nn.gelu(z[:, :2 * D_A])
    u = za[:, :D_A]
    vn = _rms(za[:, D_A:], gv_ref[...]).astype(BF16)
    lane = lax.broadcasted_iota(jnp.int32, (CHUNK, 2 * HEAD_DIM), 1)
    is_lo = lane < HEAD_DIM
    zero = jnp.zeros((CHUNK, 2 * HEAD_DIM), BF16)
    tm = x.shape[0]
    for c2 in range(tm // (2 * CHUNK)):
        for j in range(H_A // 2):
            cols = []
            for cc in range(2):
                r0 = (2 * c2 + cc) * CHUNK
                vp = vn[r0:r0 + CHUNK, j * 128:(j + 1) * 128]
                cols.append(jnp.concatenate([jnp.where(is_lo, vp, zero), jnp.where(is_lo, zero, vp)], axis=0))
            rhs = jnp.concatenate(cols, axis=1)
            sv = _dot(wsp_ref[j], rhs)
            for cc in range(2):
                r0 = (2 * c2 + cc) * CHUNK
                ybuf[r0:r0 + CHUNK, j * 128:(j + 1) * 128] = u[r0:r0 + CHUNK, j * 128:(j + 1) * 128] * (
                    sv[:, cc * 128:(cc + 1) * 128] + bsb_ref[:, j * 128:(j + 1) * 128])
    ya_ref[...] = _rms(ybuf[...], ona_ref[...]).astype(BF16)


def _in_proj(xs, n, nm, win, gv, wsp, bsb, ona):
    const = lambda i: (0, 0)
    n_a, x_specs = _token_specs(xs)
    return pl.pallas_call(
        functools.partial(_in_proj_kernel, n_a),
        grid=(n // TM_TOK,),
        in_specs=x_specs + [
            pl.BlockSpec((1, D_MODEL), const),
            pl.BlockSpec((D_MODEL, D_IN), const),
            pl.BlockSpec((1, D_A), const),
            pl.BlockSpec((H_A // 2, CHUNK, 2 * CHUNK), lambda i: (0, 0, 0)),
            pl.BlockSpec((CHUNK, D_A), const),
            pl.BlockSpec((1, D_A), const),
        ],
        out_specs=[
            pl.BlockSpec((TM_TOK, D_A), lambda i: (i, 0)),
            pl.BlockSpec((TM_TOK, 3 * D_B), lambda i: (i, 0)),
        ],
        out_shape=[jax.ShapeDtypeStruct((n, D_A), BF16), jax.ShapeDtypeStruct((n, 3 * D_B), BF16)],
        scratch_shapes=[pltpu.VMEM((TM_TOK, D_A), F32)],
        compiler_params=pltpu.CompilerParams(dimension_semantics=("arbitrary",), vmem_limit_bytes=VMEM_LIMIT),
        name="in_proj",
    )(*xs, nm, win, gv, wsp, bsb, ona)


def _natten_kernel(rows, q_ref, kp_ref, kc_ref, kn_ref, vp_ref, vc_ref, vn_ref, bias_ref, onb_ref, o_ref,
                   kwin, vwin, obuf, sbuf0, sbuf1, pbuf0, pbuf1, lbuf0, lbuf1):
    i = pl.program_id(1)
    for b, (k_ref, v_ref) in enumerate(((kp_ref, vp_ref), (kc_ref, vc_ref), (kn_ref, vn_ref))):
        kwin[b * TQ:(b + 1) * TQ, :] = k_ref[...]
        vwin[b * TQ:(b + 1) * TQ, :] = v_ref[...]
    lane = lax.broadcasted_iota(jnp.int32, (GRID_W, 128), 1)
    is_lo = lane < HEAD_DIM
    zero = jnp.zeros((GRID_W, 128), BF16)
    nkey = WIN_H * GRID_W
    n_pair = H_B // 2
    sbuf, pbuf, lbuf = (sbuf0, sbuf1), (pbuf0, pbuf1), (lbuf0, lbuf1)

    def window(rr):
        r = i * ROWS_BLK + rr
        rs = jnp.clip(r - WIN_H // 2, 0, rows - WIN_H)
        return r - rs, pl.multiple_of((rs - (i - 1) * ROWS_BLK) * GRID_W, GRID_W)

    def scores(rr, j):
        d, start = window(rr)
        cs = slice(j * 128, (j + 1) * 128)
        q2 = q_ref[rr * GRID_W:(rr + 1) * GRID_W, cs]
        qs = jnp.concatenate([jnp.where(is_lo, q2, zero), jnp.where(is_lo, zero, q2)], axis=0)
        sbuf[rr % 2][j] = _dot_nt(qs, kwin[pl.ds(start, nkey), cs]) + bias_ref[d, j]

    def softmax(rr, j):
        s = sbuf[rr % 2][j]
        p = jnp.exp(s - jnp.max(s, axis=-1, keepdims=True))
        pbuf[rr % 2][j] = p.astype(BF16)
        lbuf[rr % 2][j] = jnp.broadcast_to(1.0 / jnp.sum(p, axis=-1, keepdims=True), (2 * GRID_W, 128))

    def values(rr, j):
        _, start = window(rr)
        cs = slice(j * 128, (j + 1) * 128)
        o = _dot(pbuf[rr % 2][j], vwin[pl.ds(start, nkey), cs]) * lbuf[rr % 2][j]
        obuf[rr * GRID_W:(rr + 1) * GRID_W, cs] = jnp.where(is_lo, o[:GRID_W], o[GRID_W:])

    for t in range(ROWS_BLK + 2):
        for j in range(n_pair):
            if t < ROWS_BLK:
                scores(t, j)
            if 1 <= t <= ROWS_BLK:
                softmax(t - 1, j)
            if t >= 2:
                values(t - 2, j)
    o_ref[...] = _rms(obuf[...], onb_ref[...]).astype(BF16)


def _natten(qkv, bias, onb, n_seq, seq_len):
    rows = seq_len // GRID_W
    nblk = rows // ROWS_BLK
    blk = lambda col, f: pl.BlockSpec((TQ, D_B), lambda b, i: (b * nblk + f(i), col))
    prev = lambda i: jnp.maximum(i - 1, 0)
    cur = lambda i: i
    nxt = lambda i: jnp.minimum(i + 1, nblk - 1)
    return pl.pallas_call(
        functools.partial(_natten_kernel, rows),
        grid=(n_seq, nblk),
        in_specs=[
            blk(0, cur),
            blk(1, prev), blk(1, cur), blk(1, nxt),
            blk(2, prev), blk(2, cur), blk(2, nxt),
            pl.BlockSpec(memory_space=pltpu.VMEM),
            pl.BlockSpec((1, D_B), lambda b, i: (0, 0)),
        ],
        out_specs=pl.BlockSpec((TQ, D_B), lambda b, i: (b * nblk + i, 0)),
        out_shape=jax.ShapeDtypeStruct((n_seq * seq_len, D_B), BF16),
        scratch_shapes=[
            pltpu.VMEM((3 * TQ, D_B), BF16),
            pltpu.VMEM((3 * TQ, D_B), BF16),
            pltpu.VMEM((TQ, D_B), F32),
            pltpu.VMEM((H_B // 2, 2 * GRID_W, WIN_H * GRID_W), F32),
            pltpu.VMEM((H_B // 2, 2 * GRID_W, WIN_H * GRID_W), F32),
            pltpu.VMEM((H_B // 2, 2 * GRID_W, WIN_H * GRID_W), BF16),
            pltpu.VMEM((H_B // 2, 2 * GRID_W, WIN_H * GRID_W), BF16),
            pltpu.VMEM((H_B // 2, 2 * GRID_W, 128), F32),
            pltpu.VMEM((H_B // 2, 2 * GRID_W, 128), F32),
        ],
        compiler_params=pltpu.CompilerParams(dimension_semantics=("arbitrary", "arbitrary"),
                                             vmem_limit_bytes=VMEM_LIMIT),
        name="natten",
    )(qkv, qkv, qkv, qkv, qkv, qkv, qkv, bias, onb)


def _natten_bias(rpb):
    c = np.arange(GRID_W)
    cs = np.clip(c - WIN_W // 2, 0, GRID_W - WIN_W)
    col_mask = (c[None, :] >= cs[:, None]) & (c[None, :] < cs[:, None] + WIN_W)
    col_rel = np.clip(c[None, :] - c[:, None], -(WIN_W - 1), WIN_W - 1) + (WIN_W - 1)
    d = np.arange(WIN_H)
    row_rel = np.arange(WIN_H)[None, :] - d[:, None] + (WIN_H - 1)
    row_sel = np.eye(2 * WIN_H - 1, dtype=np.float32)[row_rel]
    col_sel = np.eye(2 * WIN_W - 1, dtype=np.float32)[col_rel]
    b = jnp.einsum("dia,ckb,hab->dhcik", row_sel, col_sel, rpb.astype(F32), precision=lax.Precision.HIGHEST)
    b = jnp.where(col_mask[None, None, :, None, :], b, NEG)
    return b.reshape(WIN_H, H_B // 2, 2 * GRID_W, WIN_H * GRID_W)


def _out_proj_kernel(n_a, ya_ref, yb_ref, *refs):
    xr, refs = _load_tokens(n_a, refs)
    woa_ref, wob_ref, nf_ref, rwt_ref, rb_ref, x2_ref, ids_ref = refs
    y = _dot(ya_ref[...], woa_ref[...]) + _dot(yb_ref[...], wob_ref[...])
    x2 = xr + y
    _store_tiled(x2_ref, x2, TM_TOK)
    xn = _rms(x2, nf_ref[...])
    lt = _dot_nt(rwt_ref[...], xn.astype(BF16)) + rb_ref[...]
    pj = [lt[8 * j:8 * (j + 1), :] for j in range(E_PER_GROUP)]
    m8 = jnp.maximum(jnp.maximum(pj[0], pj[1]), jnp.maximum(pj[2], pj[3]))
    m = jnp.max(m8, axis=0, keepdims=True)
    e = [jnp.exp(p - m) for p in pj]
    gs = None
    for a, b in PAIRS:
        s = e[a] + e[b]
        gs = s if gs is None else jnp.maximum(gs, s)
    best = gs[0:1, :]
    gsel = jnp.zeros(best.shape, jnp.int32)
    for g in range(1, N_GROUPS):
        cand = gs[g:g + 1, :]
        upd = cand > best
        gsel = jnp.where(upd, g, gsel)
        best = jnp.where(upd, cand, best)
    es = []
    for j in range(E_PER_GROUP):
        v = e[j][0:1, :]
        for g in range(1, N_GROUPS):
            v = jnp.where(gsel == g, e[j][g:g + 1, :], v)
        es.append(v)
    v1 = es[0]
    i1 = jnp.zeros(v1.shape, jnp.int32)
    for j in range(1, E_PER_GROUP):
        upd = es[j] > v1
        i1 = jnp.where(upd, j, i1)
        v1 = jnp.where(upd, es[j], v1)
    v2 = jnp.full(v1.shape, -1.0, F32)
    i2 = jnp.zeros(v1.shape, jnp.int32)
    for j in range(E_PER_GROUP):
        upd = (i1 != j) & (es[j] > v2)
        i2 = jnp.where(upd, j, i2)
        v2 = jnp.where(upd, es[j], v2)
    a = jnp.minimum(i1, i2)
    b = jnp.maximum(i1, i2)
    base = jnp.where(a == 0, 0, jnp.where(a == 1, 3, 5))
    cls = gsel * N_PAIRS + base + (b - a - 1)
    ids_ref[...] = jnp.concatenate([cls, jnp.zeros((SUBLANES - 1,) + cls.shape[1:], jnp.int32)], axis=0)


def _out_proj(ya, yb, xrs, woa, wob, nf, rwt, rb):
    n = ya.shape[0]
    const = lambda i: (0, 0)
    n_a, x_specs = _token_specs(xrs)
    return pl.pallas_call(
        functools.partial(_out_proj_kernel, n_a),
        grid=(n // TM_TOK,),
        in_specs=[
            pl.BlockSpec((TM_TOK, D_A), lambda i: (i, 0)),
            pl.BlockSpec((TM_TOK, D_B), lambda i: (i, 0)),
        ] + x_specs + [
            pl.BlockSpec((D_A, D_MODEL), const),
            pl.BlockSpec((D_B, D_MODEL), const),
            pl.BlockSpec((1, D_MODEL), const),
            pl.BlockSpec((8 * E_PER_GROUP, D_MODEL), const),
            pl.BlockSpec((8 * E_PER_GROUP, 1), const),
        ],
        out_specs=[
            pl.BlockSpec((TM_TOK * SUBLANES, LANES), lambda i: (i, 0)),
            pl.BlockSpec((SUBLANES, TM_TOK), lambda i: (0, i)),
        ],
        out_shape=[
            jax.ShapeDtypeStruct((n * SUBLANES, LANES), F32),
            jax.ShapeDtypeStruct((SUBLANES, n), jnp.int32),
        ],
        compiler_params=pltpu.CompilerParams(dimension_semantics=("arbitrary",), vmem_limit_bytes=VMEM_LIMIT),
        name="out_proj",
    )(ya, yb, *xrs, woa, wob, nf, rwt, rb)


def _moe_kernel(n_tok, ea_ref, eb_ref, row0_ref, cnt_ref, order_ref, nf_ref, rwa_ref, rwb_ref, rba_ref, rbb_ref,
                wga_ref, wua_ref, wda_ref, wgb_ref, wub_ref, wdb_ref, x2_hbm, x3_hbm,
                xbuf, obuf, gsem, ssem):
    j = pl.program_id(0)
    n = pl.num_programs(0)
    slot = j % 2
    unroll = 8

    def tile_of(tok):
        return pl.ds(pl.multiple_of(tok * SUBLANES, SUBLANES), SUBLANES)

    def gather_rows(t, s):
        base = row0_ref[t]

        def body(i, c):
            for k in range(unroll):
                r = i * unroll + k
                pltpu.make_async_copy(x2_hbm.at[tile_of(order_ref[base + r])], xbuf.at[s, tile_of(r)],
                                      gsem.at[s]).start()
            return c
        lax.fori_loop(0, TM_MOE // unroll, body, 0)

    def wait_gather(s):
        pltpu.make_async_copy(x2_hbm.at[pl.ds(0, TM_MOE * SUBLANES)], xbuf.at[s], gsem.at[s]).wait()

    def scatter_rows(t, s):
        base = row0_ref[t]
        cnt = cnt_ref[t]
        pad0 = n_tok + t * TM_MOE - base - cnt

        def rows(dst_of):
            def body(i, c):
                for k in range(unroll):
                    r = i * unroll + k
                    pltpu.make_async_copy(obuf.at[s, tile_of(r)], x3_hbm.at[tile_of(dst_of(r))],
                                          ssem.at[s]).start()
                return c
            lax.fori_loop(0, TM_MOE // unroll, body, 0)

        @pl.when(cnt == TM_MOE)
        def _():
            rows(lambda r: order_ref[base + r])

        @pl.when(cnt < TM_MOE)
        def _():
            rows(lambda r: jnp.where(r < cnt, order_ref[base + r], pad0 + r))

    def wait_scatter(s):
        pltpu.make_async_copy(obuf.at[s], x3_hbm.at[pl.ds(0, TM_MOE * SUBLANES)], ssem.at[s]).wait()

    @pl.when(j == 0)
    def _():
        gather_rows(0, 0)

    wait_gather(slot)

    @pl.when(j + 1 < n)
    def _():
        gather_rows(j + 1, 1 - slot)

    @pl.when(j >= 2)
    def _():
        wait_scatter(slot)

    x = _load_tiled(xbuf.at[slot], TM_MOE)
    xn = _rms(x, nf_ref[...])
    la = jnp.sum(xn * rwa_ref[...], axis=-1, keepdims=True) + rba_ref[:, 0:1]
    lb = jnp.sum(xn * rwb_ref[...], axis=-1, keepdims=True) + rbb_ref[:, 0:1]
    lm = jnp.maximum(la, lb)
    pa = jnp.exp(la - lm)
    pb = jnp.exp(lb - lm)
    tot = pa + pb
    xb = xn.astype(BF16)
    ha = jax.nn.silu(_dot(xb, wga_ref[...])) * _dot(xb, wua_ref[...])
    oa = _dot(ha.astype(BF16), wda_ref[...])
    hb = jax.nn.silu(_dot(xb, wgb_ref[...])) * _dot(xb, wub_ref[...])
    ob = _dot(hb.astype(BF16), wdb_ref[...])
    _store_tiled(obuf.at[slot], x + (pa / tot) * oa + (pb / tot) * ob, TM_MOE)
    scatter_rows(j, slot)

    @pl.when(j == n - 1)
    def _():
        wait_scatter(slot)

        @pl.when(n >= 2)
        def _():
            wait_scatter(1 - slot)


def _moe(x2, plan, nf, rw_rows, rb_rows, wg, wu, wd, n):
    tile_ea, tile_eb, row0, cnt, order = plan
    n_tiles = tile_ea.shape[0]
    n_rows_out = n_tiles * TM_MOE
    wspec = lambda shape, which: pl.BlockSpec(
        (None,) + shape, lambda j, ea, eb, r0, ct, od: ((ea, eb)[which][j], 0, 0))
    grid_spec = pltpu.PrefetchScalarGridSpec(
        num_scalar_prefetch=5,
        grid=(n_tiles,),
        in_specs=[
            pl.BlockSpec((1, D_MODEL), lambda j, *_: (0, 0)),
            wspec((1, D_MODEL), 0), wspec((1, D_MODEL), 1), wspec((1, LANES), 0), wspec((1, LANES), 1),
            wspec((D_MODEL, D_EXPERT), 0), wspec((D_MODEL, D_EXPERT), 0), wspec((D_EXPERT, D_MODEL), 0),
            wspec((D_MODEL, D_EXPERT), 1), wspec((D_MODEL, D_EXPERT), 1), wspec((D_EXPERT, D_MODEL), 1),
            pl.BlockSpec(memory_space=pl.ANY),
        ],
        out_specs=pl.BlockSpec(memory_space=pl.ANY),
        scratch_shapes=[
            pltpu.VMEM((2, TM_MOE * SUBLANES, LANES), F32),
            pltpu.VMEM((2, TM_MOE * SUBLANES, LANES), F32),
            pltpu.SemaphoreType.DMA((2,)),
            pltpu.SemaphoreType.DMA((2,)),
        ],
    )
    return pl.pallas_call(
        functools.partial(_moe_kernel, n),
        grid_spec=grid_spec,
        out_shape=jax.ShapeDtypeStruct((n_rows_out * SUBLANES, LANES), F32),
        compiler_params=pltpu.CompilerParams(dimension_semantics=("arbitrary",), vmem_limit_bytes=VMEM_LIMIT),
        name="moe",
    )(tile_ea, tile_eb, row0, cnt, order, nf, rw_rows, rw_rows, rb_rows, rb_rows, wg, wu, wd, wg, wu, wd, x2)


def _moe_plan(cls, n):
    n_tiles = n // TM_MOE + N_CLASSES
    _, order = lax.sort((cls, lax.iota(jnp.int32, n)), num_keys=1, is_stable=True)
    counts = jnp.sum((cls[None, :] == jnp.arange(N_CLASSES, dtype=jnp.int32)[:, None]).astype(jnp.int32), axis=1)
    tiles_per = (counts + TM_MOE - 1) // TM_MOE
    tile_end = jnp.cumsum(tiles_per)
    sorted_start = jnp.cumsum(counts) - counts
    t = jnp.arange(n_tiles, dtype=jnp.int32)
    tile_cls = jnp.minimum(jnp.sum((tile_end[None, :] <= t[:, None]).astype(jnp.int32), axis=1), N_CLASSES - 1)
    k = t - (tile_end - tiles_per)[tile_cls]
    cnt = jnp.clip(counts[tile_cls] - k * TM_MOE, 0, TM_MOE)
    row0 = jnp.minimum(sorted_start[tile_cls] + k * TM_MOE, n)
    row0 = jnp.where(cnt > 0, row0, jnp.minimum(sorted_start[tile_cls] + counts[tile_cls], n))
    pair_a = jnp.array([p[0] for p in PAIRS], jnp.int32)
    pair_b = jnp.array([p[1] for p in PAIRS], jnp.int32)
    tile_ea = (tile_cls // N_PAIRS) * E_PER_GROUP + pair_a[tile_cls % N_PAIRS]
    tile_eb = (tile_cls // N_PAIRS) * E_PER_GROUP + pair_b[tile_cls % N_PAIRS]
    i32 = lambda v: v.astype(jnp.int32)
    order = jnp.concatenate([order, jnp.zeros((TM_MOE,), jnp.int32)])
    return i32(tile_ea), i32(tile_eb), i32(row0), i32(cnt), i32(order)


def _final_norm_kernel(x_ref, g_ref, o_ref):
    o_ref[...] = _rms(_load_tiled(x_ref, TM_TOK), g_ref[...])


def _final_norm(x, g, row0, n_rows):
    blk0 = row0 // TM_TOK
    return pl.pallas_call(
        _final_norm_kernel,
        grid=(n_rows // TM_TOK,),
        in_specs=[pl.BlockSpec((TM_TOK * SUBLANES, LANES), lambda i: (blk0 + i, 0)),
                  pl.BlockSpec((1, D_MODEL), lambda i: (0, 0))],
        out_specs=pl.BlockSpec((TM_TOK, D_MODEL), lambda i: (i, 0)),
        out_shape=jax.ShapeDtypeStruct((n_rows, D_MODEL), F32),
        compiler_params=pltpu.CompilerParams(dimension_semantics=("arbitrary",), vmem_limit_bytes=VMEM_LIMIT),
        name="final_norm",
    )(x, g)


def _trunk(xs, n_seq, seq_len, norm_mix, w_in, gmlp_v_norm, gmlp_ws, gmlp_bs, na_rpb, out_norm_a, out_norm_b,
           w_out, norm_ffn, router_w, router_bias, w_gate, w_up, w_down):
    n = n_seq * seq_len
    row = lambda v: v.reshape(1, -1).astype(F32)
    perm = np.array([[E_PER_GROUP * g + j for g in range(N_GROUPS)] for j in range(E_PER_GROUP)])
    rwt = jnp.zeros((E_PER_GROUP, 8, D_MODEL), F32).at[:, :N_GROUPS, :].set(jnp.transpose(router_w)[perm])
    rwt = rwt.reshape(8 * E_PER_GROUP, D_MODEL).astype(BF16)
    rb = jnp.full((E_PER_GROUP, 8), NEG, F32).at[:, :N_GROUPS].set(router_bias.astype(F32)[perm])
    rb = rb.reshape(8 * E_PER_GROUP, 1)
    rw_rows = jnp.transpose(router_w).astype(F32).reshape(N_EXPERTS, 1, D_MODEL)
    rb_rows = jnp.broadcast_to(router_bias.astype(F32).reshape(N_EXPERTS, 1, 1), (N_EXPERTS, 1, LANES))
    for l in range(DEPTH):
        ws = gmlp_ws[l].astype(BF16)
        wsp = jnp.concatenate([ws[0::2], ws[1::2]], axis=2)
        bsb = jnp.repeat(jnp.transpose(gmlp_bs[l]).astype(F32), HEAD_DIM, axis=1)
        ya, qkv = _in_proj(xs, n, row(norm_mix[l]), w_in[l].astype(BF16), row(gmlp_v_norm[l]), wsp, bsb,
                           row(out_norm_a[l]))
        yb = _natten(qkv, _natten_bias(na_rpb[l]), row(out_norm_b[l]), n_seq, seq_len)
        wo = w_out[l].astype(BF16)
        x2, ids = _out_proj(ya, yb, xs, wo[:D_A], wo[D_A:], row(norm_ffn[l]), rwt, rb)
        xs = [_moe(x2, _moe_plan(ids[0], n), row(norm_ffn[l]), rw_rows, rb_rows, w_gate[l].astype(BF16),
                   w_up[l].astype(BF16), w_down[l].astype(BF16), n)]
    return xs[0]


def kernel(x_prompt, x_sample, norm_mix, w_in, gmlp_v_norm, gmlp_ws, gmlp_bs, na_rpb, out_norm_a, out_norm_b,
           w_out, norm_ffn, router_w, router_bias, w_gate, w_up, w_down, norm_final):
    bp, sp, _ = x_prompt.shape
    bs, ss, _ = x_sample.shape
    assert sp == ss and (bp * sp) % TM_TOK == 0
    xs = [x_prompt.reshape(bp * sp, D_MODEL), x_sample.reshape(bs * ss, D_MODEL)]
    x = _trunk(xs, bp + bs, sp, norm_mix, w_in, gmlp_v_norm, gmlp_ws, gmlp_bs, na_rpb, out_norm_a, out_norm_b,
               w_out, norm_ffn, router_w, router_bias, w_gate, w_up, w_down)
    g = norm_final.reshape(1, -1).astype(F32)
    y_prompt = _final_norm(x, g, 0, bp * sp).reshape(bp, sp, D_MODEL)
    y_sample = _final_norm(x, g, bp * sp, bs * ss).reshape(bs, ss, D_MODEL)
    return (y_prompt, y_sample)
```
